```python
import jax
import jax.numpy as jnp
from jax import lax
import numpy as np

D_MODEL = 2048
BATCH = 4
SEQ = 4096
DEPTH = 2

N_EVEN = (DEPTH + 1) // 2
N_ODD = DEPTH // 2
CHUNK = 128
CONV_W = 4
EPS = 1e-6
F32 = jnp.float32

RET_HEADS = 8
RET_DIM = 128
RET_WIDTH = RET_HEADS * RET_DIM
ROPE_BASE = 10000.0
SSD_HEAD_DIM = 64
SSD_HEADS = 16
SSD_WIDTH = SSD_HEADS * SSD_HEAD_DIM
SSD_GROUPS = 2
SSD_HPG = SSD_HEADS // SSD_GROUPS
SSD_STATE = 128
SSD_CONV_CH = SSD_WIDTH + 2 * SSD_GROUPS * SSD_STATE
IN0_SPLITS = [RET_WIDTH, 2 * RET_WIDTH, 3 * RET_WIDTH, 4 * RET_WIDTH, 4 * RET_WIDTH + SSD_WIDTH, 4 * RET_WIDTH + SSD_WIDTH + SSD_CONV_CH]
IN0_WIDTH = IN0_SPLITS[-1] + SSD_HEADS
MIX0_WIDTH = RET_WIDTH + SSD_WIDTH
MLSTM_HEADS = 4
MLSTM_DIM = 384
MLSTM_WIDTH = MLSTM_HEADS * MLSTM_DIM
S5_GROUP = 16
S5_GROUPS = 32
S5_WIDTH = S5_GROUP * S5_GROUPS
S5_STATE = 64
IN1_SPLITS = [MLSTM_WIDTH, 2 * MLSTM_WIDTH, 3 * MLSTM_WIDTH, 4 * MLSTM_WIDTH, 4 * MLSTM_WIDTH + MLSTM_HEADS, 4 * MLSTM_WIDTH + 2 * MLSTM_HEADS]
IN1_WIDTH = IN1_SPLITS[-1] + S5_WIDTH
MIX1_WIDTH = MLSTM_WIDTH + S5_WIDTH
MOE_GROUPS = 4
EXPERTS_PER_GROUP = 8
N_EXPERTS = MOE_GROUPS * EXPERTS_PER_GROUP
MOE_TOP_K = 2
EXPERT_FF = 512
MOE_BLOCK = 128

kernel_name = 'hybrid_ret_ssd_mlstm_s5_hmoe'


def rms_norm(x, g):
    xf = x.astype(F32)
    y = xf * lax.rsqrt(jnp.mean(xf * xf, axis=-1, keepdims=True) + EPS)
    return (y * g.astype(F32)).astype(x.dtype)


def head_layer_norm(t, g):
    tf = t.astype(F32)
    mu = jnp.mean(tf, axis=-1, keepdims=True)
    var = jnp.mean(jnp.square(tf - mu), axis=-1, keepdims=True)
    return (tf - mu) * lax.rsqrt(var + EPS) * g.astype(F32)


def causal_dwconv(x, w, b):
    y = lax.conv_general_dilated(x, w[:, None, :], window_strides=(1,), padding=[(CONV_W - 1, 0)], dimension_numbers=('NWC', 'WIO', 'NWC'), feature_group_count=x.shape[-1])
    return y + b


def to_chunks(t):
    b, l = t.shape[:2]
    return jnp.moveaxis(t.reshape((b, l // CHUNK, CHUNK) + t.shape[2:]), 1, 0)


def from_chunks(t):
    c, b = t.shape[:2]
    return jnp.moveaxis(t, 0, 1).reshape((b, c * CHUNK) + t.shape[3:])


def causal_mask():
    return jnp.tril(jnp.ones((CHUNK, CHUNK), dtype=bool))


def rotary(t):
    l, d = t.shape[1], t.shape[-1]
    inv = 1.0 / (ROPE_BASE ** (jnp.arange(0, d, 2, dtype=F32) / d))
    ang = jnp.arange(l, dtype=F32)[:, None] * inv[None, :]
    cos = jnp.cos(ang)[None, :, None, :]
    sin = jnp.sin(ang)[None, :, None, :]
    t1, t2 = jnp.split(t, 2, axis=-1)
    return jnp.concatenate([t1 * cos - t2 * sin, t1 * sin + t2 * cos], axis=-1)


def retention(q, k, v, g, norm_g):
    b = q.shape[0]
    q = rotary(q.astype(F32))
    k = rotary(k.astype(F32)) * RET_DIM ** -0.5
    v = v.astype(F32)
    log_gamma = jnp.log1p(-jnp.exp2(-5.0 - jnp.arange(RET_HEADS, dtype=F32)))
    pos = jnp.arange(CHUNK, dtype=F32)
    rel = pos[:, None] - pos[None, :]
    decay = jnp.where(rel >= 0, jnp.exp(log_gamma[:, None, None] * jnp.maximum(rel, 0.0)), 0.0)
    q_decay = jnp.exp(log_gamma[None, :] * (pos[:, None] + 1.0))
    k_decay = jnp.exp(log_gamma[None, :] * (CHUNK - 1.0 - pos[:, None]))
    chunk_decay = jnp.exp(log_gamma * CHUNK)

    def step(state, inp):
        qc, kc, vc = inp
        scores = jnp.einsum('blhd,bshd->bhls', qc, kc) * decay
        intra = jnp.einsum('bhls,bshe->blhe', scores, vc)
        inter = jnp.einsum('blhd,bhde->blhe', qc, state) * q_decay[None, :, :, None]
        new_state = chunk_decay[None, :, None, None] * state + jnp.einsum('bshd,sh,bshe->bhde', kc, k_decay, vc)
        return new_state, intra + inter

    state0 = jnp.zeros((b, RET_HEADS, RET_DIM, RET_DIM), F32)
    _, out = lax.scan(step, state0, (to_chunks(q), to_chunks(k), to_chunks(v)))
    return jax.nn.silu(g.astype(F32)) * head_layer_norm(from_chunks(out), norm_g)


def ssd(xbc, z, dt_raw, conv_w, conv_b, dt_bias, a_log, d_skip, norm_g):
    b, l, _ = z.shape
    xbc = jax.nn.silu(causal_dwconv(xbc, conv_w, conv_b)).astype(F32)
    gn = SSD_GROUPS * SSD_STATE
    xs = xbc[..., :SSD_WIDTH].reshape(b, l, SSD_GROUPS, SSD_HPG, SSD_HEAD_DIM)
    bm = xbc[..., SSD_WIDTH:SSD_WIDTH + gn].reshape(b, l, SSD_GROUPS, SSD_STATE)
    cm = xbc[..., SSD_WIDTH + gn:].reshape(b, l, SSD_GROUPS, SSD_STATE)
    dt = jax.nn.softplus(dt_raw.astype(F32) + dt_bias.astype(F32)).reshape(b, l, SSD_GROUPS, SSD_HPG)
    da = dt * (-jnp.exp(a_log.astype(F32))).reshape(SSD_GROUPS, SSD_HPG)
    xdt = xs * dt[..., None]
    mask = causal_mask()[None, :, :, None, None]

    def step(state, inp):
        xc, dac, bc, cc = inp
        a_cum = jnp.cumsum(dac, axis=1)
        seg = jnp.exp(jnp.where(mask, a_cum[:, :, None] - a_cum[:, None, :], -jnp.inf))
        cb = jnp.einsum('blgn,bsgn->blsg', cc, bc)
        intra = jnp.einsum('blsg,blsgr,bsgrp->blgrp', cb, seg, xc)
        inter = jnp.einsum('blgn,bgrpn->blgrp', cc, state) * jnp.exp(a_cum)[..., None]
        to_end = jnp.exp(a_cum[:, -1:] - a_cum)
        new_state = jnp.exp(a_cum[:, -1])[..., None, None] * state + jnp.einsum('bsgn,bsgr,bsgrp->bgrpn', bc, to_end, xc)
        return new_state, intra + inter

    state0 = jnp.zeros((b, SSD_GROUPS, SSD_HPG, SSD_HEAD_DIM, SSD_STATE), F32)
    _, y = lax.scan(step, state0, (to_chunks(xdt), to_chunks(da), to_chunks(bm), to_chunks(cm)))
    y = from_chunks(y) + d_skip.astype(F32).reshape(SSD_GROUPS, SSD_HPG)[..., None] * xs
    y = y.reshape(b, l, SSD_WIDTH) * jax.nn.silu(z.astype(F32))
    y = y.reshape(b, l, SSD_GROUPS, SSD_WIDTH // SSD_GROUPS)
    y = y * lax.rsqrt(jnp.mean(y * y, axis=-1, keepdims=True) + EPS)
    return y.reshape(b, l, SSD_WIDTH) * norm_g.astype(F32)


def mlstm(q, k, v, o, i_pre, f_pre, norm_g):
    b = q.shape[0]
    q = q.astype(F32) * MLSTM_DIM ** -0.5
    k = k.astype(F32)
    v = v.astype(F32)
    ig = i_pre.astype(F32)
    lf = jax.nn.log_sigmoid(f_pre.astype(F32))
    mask = causal_mask()[None, :, :, None]

    def step(carry, inp):
        c_st, n_st, m_st = carry
        qc, kc, vc, igc, lfc = inp
        bcum = jnp.cumsum(lfc, axis=1)
        log_d = jnp.where(mask, bcum[:, :, None, :] - bcum[:, None, :, :] + igc[:, None, :, :], -jnp.inf)
        m_inter = bcum + m_st[:, None, :]
        m_t = jnp.maximum(jnp.max(log_d, axis=2), m_inter)
        scores = jnp.einsum('blhd,bshd->blsh', qc, kc) * jnp.exp(log_d - m_t[:, :, None, :])
        inter_scale = jnp.exp(m_inter - m_t)
        num = jnp.einsum('blsh,bshe->blhe', scores, vc) + inter_scale[..., None] * jnp.einsum('blhd,bhde->blhe', qc, c_st)
        den = jnp.sum(scores, axis=2) + inter_scale * jnp.einsum('blhd,bhd->blh', qc, n_st)
        h = num / jnp.maximum(jnp.abs(den), jnp.exp(-m_t))[..., None]
        b_last = bcum[:, -1]
        w_log = b_last[:, None] - bcum + igc
        m_new = jnp.maximum(b_last + m_st, jnp.max(w_log, axis=1))
        w = jnp.exp(w_log - m_new[:, None])
        prev_scale = jnp.exp(b_last + m_st - m_new)
        c_new = prev_scale[..., None, None] * c_st + jnp.einsum('bsh,bshd,bshe->bhde', w, kc, vc)
        n_new = prev_scale[..., None] * n_st + jnp.einsum('bsh,bshd->bhd', w, kc)
        return (c_new, n_new, m_new), h

    carry0 = (jnp.zeros((b, MLSTM_HEADS, MLSTM_DIM, MLSTM_DIM), F32), jnp.zeros((b, MLSTM_HEADS, MLSTM_DIM), F32), jnp.full((b, MLSTM_HEADS), -jnp.inf, F32))
    _, h = lax.scan(step, carry0, (to_chunks(q), to_chunks(k), to_chunks(v), to_chunks(ig), to_chunks(lf)))
    return jax.nn.sigmoid(o.astype(F32)) * head_layer_norm(from_chunks(h), norm_g)


def complex_affine_combine(first, second):
    a1r, a1i, b1r, b1i = first
    a2r, a2i, b2r, b2i = second
    return (a2r * a1r - a2i * a1i, a2r * a1i + a2i * a1r, a2r * b1r - a2i * b1i + b2r, a2r * b1i + a2i * b1r + b2i)


def s5(u, a_re, a_im, log_dt, b_re, b_im, c_re, c_im, d_skip, w_glu, b_glu):
    bsz, l, _ = u.shape
    uf = u.astype(F32).reshape(bsz, l, S5_GROUPS, S5_GROUP)
    a_re = a_re.astype(F32)
    a_im = a_im.astype(F32)
    dt = jnp.exp(log_dt.astype(F32))[:, None]
    mag = jnp.exp(a_re * dt)
    abar_re = mag * jnp.cos(a_im * dt)
    abar_im = mag * jnp.sin(a_im * dt)
    den = a_re * a_re + a_im * a_im
    nr = abar_re - 1.0
    coef_re = (nr * a_re + abar_im * a_im) / den
    coef_im = (abar_im * a_re - nr * a_im) / den
    b_re = b_re.astype(F32)
    b_im = b_im.astype(F32)
    bbar_re = coef_re[..., None] * b_re - coef_im[..., None] * b_im
    bbar_im = coef_re[..., None] * b_im + coef_im[..., None] * b_re
    bu_re = jnp.einsum('gnc,blgc->blgn', bbar_re, uf)
    bu_im = jnp.einsum('gnc,blgc->blgn', bbar_im, uf)
    elems = (jnp.broadcast_to(abar_re, bu_re.shape), jnp.broadcast_to(abar_im, bu_im.shape), bu_re, bu_im)
    _, _, s_re, s_im = lax.associative_scan(complex_affine_combine, elems, axis=1)
    y = jnp.einsum('gcn,blgn->blgc', c_re.astype(F32), s_re) - jnp.einsum('gcn,blgn->blgc', c_im.astype(F32), s_im) + d_skip.astype(F32) * uf
    y = jax.nn.gelu(y.reshape(bsz, l, S5_WIDTH))
    val, gate = jnp.split(y @ w_glu.astype(F32) + b_glu.astype(F32), 2, axis=-1)
    return val * jax.nn.sigmoid(gate)


def expert_ffn(xb, wg, wu, wd):
    return (jax.nn.silu(xb @ wg) * (xb @ wu)) @ wd


def hier_moe(x, router_g, router_g_b, router_e, router_e_b, w_gate, w_up, w_down):
    b, l, d = x.shape
    xf = x.reshape(-1, d)
    t = xf.shape[0]
    g_prob = jax.nn.softmax((xf @ router_g + router_g_b).astype(F32), axis=-1)
    g_w, g_idx = lax.top_k(g_prob, 1)
    e_logits = (xf @ router_e + router_e_b).astype(F32).reshape(t, MOE_GROUPS, EXPERTS_PER_GROUP)
    sel = jnp.broadcast_to(g_idx[:, :, None], (t, 1, EXPERTS_PER_GROUP))
    e_in_group = jnp.take_along_axis(e_logits, sel, axis=1)[:, 0]
    top_logit, top_j = lax.top_k(e_in_group, MOE_TOP_K)
    top_w = jax.nn.softmax(top_logit, axis=-1) * g_w
    flat_e = (g_idx * EXPERTS_PER_GROUP + top_j).reshape(-1)
    n_slots = t * MOE_TOP_K
    order = jnp.argsort(flat_e)
    sorted_e = flat_e[order]
    counts = jnp.bincount(flat_e, length=N_EXPERTS)
    padded = (counts + MOE_BLOCK - 1) // MOE_BLOCK * MOE_BLOCK
    pad_end = jnp.cumsum(padded)
    pad_start = pad_end - padded
    start = jnp.cumsum(counts) - counts
    dest = pad_start[sorted_e] + jnp.arange(n_slots) - start[sorted_e]
    n_blocks = (n_slots + N_EXPERTS * (MOE_BLOCK - 1) + MOE_BLOCK - 1) // MOE_BLOCK
    n_rows = n_blocks * MOE_BLOCK
    tok = order // MOE_TOP_K
    buf = jnp.zeros((n_rows, d), x.dtype).at[dest].set(xf[tok])
    block_expert = jnp.minimum(jnp.searchsorted(pad_end, jnp.arange(n_blocks) * MOE_BLOCK, side='right'), N_EXPERTS - 1)

    def run_block(args):
        xb, e = args
        return expert_ffn(xb, w_gate[e], w_up[e], w_down[e])

    out_buf = lax.map(run_block, (buf.reshape(n_blocks, MOE_BLOCK, d), block_expert)).reshape(n_rows, d)
    y_slot = out_buf[dest] * top_w.reshape(-1)[order][:, None].astype(x.dtype)
    return jax.ops.segment_sum(y_slot, tok, num_segments=t).reshape(b, l, d)


def even_mixer(h, w_in, ret_norm, conv_w, conv_b, dt_bias, a_log, d_skip, ssd_norm, w_out):
    b, l, _ = h.shape
    q, k, v, g, z, xbc, dt_raw = jnp.split(h @ w_in, IN0_SPLITS, axis=-1)

    def heads(t):
        return t.reshape(b, l, RET_HEADS, RET_DIM)

    a_out = retention(heads(q), heads(k), heads(v), heads(g), ret_norm).reshape(b, l, RET_WIDTH)
    b_out = ssd(xbc, z, dt_raw, conv_w, conv_b, dt_bias, a_log, d_skip, ssd_norm)
    mixed = jnp.concatenate([a_out, b_out], axis=-1).astype(h.dtype)
    return mixed @ w_out


def odd_mixer(h, w_in, conv_w, conv_b, i_bias, f_bias, norm_g, a_re, a_im, log_dt, b_re, b_im, c_re, c_im, d_skip, w_glu, b_glu, w_out):
    b, l, _ = h.shape
    q, k, v, o, i_pre, f_pre, u = jnp.split(h @ w_in, IN1_SPLITS, axis=-1)
    qk = jax.nn.silu(causal_dwconv(jnp.concatenate([q, k], axis=-1), conv_w, conv_b))
    q, k = jnp.split(qk, 2, axis=-1)

    def heads(t):
        return t.reshape(b, l, MLSTM_HEADS, MLSTM_DIM)

    c_out = mlstm(heads(q), heads(k), heads(v), heads(o), i_pre + i_bias, f_pre + f_bias, norm_g).reshape(b, l, MLSTM_WIDTH)
    d_out = s5(u, a_re, a_im, log_dt, b_re, b_im, c_re, c_im, d_skip, w_glu, b_glu)
    mixed = jnp.concatenate([c_out, d_out], axis=-1).astype(h.dtype)
    return mixed @ w_out


def setup_inputs(seed: int = 0) -> dict:
    key = jax.random.key(seed)
    ks = iter(jax.random.split(key, 64))

    def nrm(shape, scale):
        return jax.random.normal(next(ks), shape, F32) * scale

    def gain(shape):
        return 1.0 + nrm(shape, 0.02)

    def unif(shape, lo, hi):
        return jax.random.uniform(next(ks), shape, F32, lo, hi)

    log_dt_lo, log_dt_hi = float(np.log(1e-3)), float(np.log(1e-1))
    ssd_dt = jnp.exp(unif((N_EVEN, SSD_HEADS), log_dt_lo, log_dt_hi))
    return {
        'x': nrm((BATCH, SEQ, D_MODEL), 1.0),
        'even_norm': gain((N_EVEN, D_MODEL)),
        'even_w_in': nrm((N_EVEN, D_MODEL, IN0_WIDTH), D_MODEL ** -0.5),
        'ret_norm': gain((N_EVEN, RET_HEADS, RET_DIM)),
        'ssd_conv_w': nrm((N_EVEN, CONV_W, SSD_CONV_CH), CONV_W ** -0.5),
        'ssd_conv_b': nrm((N_EVEN, SSD_CONV_CH), 0.02),
        'ssd_dt_bias': ssd_dt + jnp.log(-jnp.expm1(-ssd_dt)),
        'ssd_a_log': jnp.log(unif((N_EVEN, SSD_HEADS), 1.0, 16.0)),
        'ssd_d': gain((N_EVEN, SSD_HEADS)),
        'ssd_norm': gain((N_EVEN, SSD_WIDTH)),
        'even_w_out': nrm((N_EVEN, MIX0_WIDTH, D_MODEL), MIX0_WIDTH ** -0.5),
        'odd_norm': gain((N_ODD, D_MODEL)),
        'odd_w_in': nrm((N_ODD, D_MODEL, IN1_WIDTH), D_MODEL ** -0.5),
        'mlstm_conv_w': nrm((N_ODD, CONV_W, 2 * MLSTM_WIDTH), CONV_W ** -0.5),
        'mlstm_conv_b': nrm((N_ODD, 2 * MLSTM_WIDTH), 0.02),
        'mlstm_i_bias': nrm((N_ODD, MLSTM_HEADS), 0.1),
        'mlstm_f_bias': jnp.linspace(3.0, 6.0, MLSTM_HEADS, dtype=F32)[None, :] + nrm((N_ODD, MLSTM_HEADS), 0.02),
        'mlstm_norm': gain((N_ODD, MLSTM_HEADS, MLSTM_DIM)),
        's5_a_re': -0.5 + nrm((N_ODD, S5_GROUPS, S5_STATE), 0.01),
        's5_a_im': jnp.pi * jnp.arange(S5_STATE, dtype=F32) + nrm((N_ODD, S5_GROUPS, S5_STATE), 0.01),
        's5_log_dt': unif((N_ODD, S5_GROUPS), log_dt_lo, log_dt_hi),
        's5_b_re': nrm((N_ODD, S5_GROUPS, S5_STATE, S5_GROUP), (2 * S5_GROUP) ** -0.5),
        's5_b_im': nrm((N_ODD, S5_GROUPS, S5_STATE, S5_GROUP), (2 * S5_GROUP) ** -0.5),
        's5_c_re': nrm((N_ODD, S5_GROUPS, S5_GROUP, S5_STATE), S5_STATE ** -0.5),
        's5_c_im': nrm((N_ODD, S5_GROUPS, S5_GROUP, S5_STATE), S5_STATE ** -0.5),
        's5_d': nrm((N_ODD, S5_GROUPS, S5_GROUP), 1.0),
        's5_w_glu': nrm((N_ODD, S5_WIDTH, 2 * S5_WIDTH), S5_WIDTH ** -0.5),
        's5_b_glu': nrm((N_ODD, 2 * S5_WIDTH), 0.02),
        'odd_w_out': nrm((N_ODD, MIX1_WIDTH, D_MODEL), MIX1_WIDTH ** -0.5),
        'moe_norm': gain((DEPTH, D_MODEL)),
        'moe_router_g': nrm((DEPTH, D_MODEL, MOE_GROUPS), D_MODEL ** -0.5),
        'moe_router_g_b': nrm((DEPTH, MOE_GROUPS), 0.01),
        'moe_router_e': nrm((DEPTH, D_MODEL, N_EXPERTS), D_MODEL ** -0.5),
        'moe_router_e_b': nrm((DEPTH, N_EXPERTS), 0.01),
        'moe_w_gate': nrm((DEPTH, N_EXPERTS, D_MODEL, EXPERT_FF), D_MODEL ** -0.5),
        'moe_w_up': nrm((DEPTH, N_EXPERTS, D_MODEL, EXPERT_FF), D_MODEL ** -0.5),
        'moe_w_down': nrm((DEPTH, N_EXPERTS, EXPERT_FF, D_MODEL), EXPERT_FF ** -0.5),
        'final_norm': gain((D_MODEL,)),
    }


def reference(x, even_norm, even_w_in, ret_norm, ssd_conv_w, ssd_conv_b, ssd_dt_bias, ssd_a_log, ssd_d, ssd_norm, even_w_out, odd_norm, odd_w_in, mlstm_conv_w, mlstm_conv_b, mlstm_i_bias, mlstm_f_bias, mlstm_norm, s5_a_re, s5_a_im, s5_log_dt, s5_b_re, s5_b_im, s5_c_re, s5_c_im, s5_d, s5_w_glu, s5_b_glu, odd_w_out, moe_norm, moe_router_g, moe_router_g_b, moe_router_e, moe_router_e_b, moe_w_gate, moe_w_up, moe_w_down, final_norm):
    h = x
    for layer in range(DEPTH):
        i = layer // 2
        if layer % 2 == 0:
            h = h + even_mixer(rms_norm(h, even_norm[i]), even_w_in[i], ret_norm[i], ssd_conv_w[i], ssd_conv_b[i], ssd_dt_bias[i], ssd_a_log[i], ssd_d[i], ssd_norm[i], even_w_out[i])
        else:
            h = h + odd_mixer(rms_norm(h, odd_norm[i]), odd_w_in[i], mlstm_conv_w[i], mlstm_conv_b[i], mlstm_i_bias[i], mlstm_f_bias[i], mlstm_norm[i], s5_a_re[i], s5_a_im[i], s5_log_dt[i], s5_b_re[i], s5_b_im[i], s5_c_re[i], s5_c_im[i], s5_d[i], s5_w_glu[i], s5_b_glu[i], odd_w_out[i])
        h = h + hier_moe(rms_norm(h, moe_norm[layer]), moe_router_g[layer], moe_router_g_b[layer], moe_router_e[layer], moe_router_e_b[layer], moe_w_gate[layer], moe_w_up[layer], moe_w_down[layer])
    return rms_norm(h, final_norm)
```

```python
import functools
import math

import numpy as np
import jax
import jax.numpy as jnp
from jax import lax
from jax.experimental import pallas as pl
from jax.experimental.pallas import tpu as pltpu

F32 = jnp.float32
BF16 = jnp.bfloat16
HIGHEST = lax.Precision.HIGHEST

D_MODEL = 2048
CHUNK = 128
CONV_W = 4
EPS = 1e-6
LANES = 128
CONV_PAD = 8

RET_HEADS = 8
RET_DIM = 128
RET_WIDTH = RET_HEADS * RET_DIM
ROPE_BASE = 10000.0
SSD_HEAD_DIM = 64
SSD_HEADS = 16
SSD_WIDTH = SSD_HEADS * SSD_HEAD_DIM
SSD_GROUPS = 2
SSD_HPG = SSD_HEADS // SSD_GROUPS
SSD_STATE = 128
SSD_GW = SSD_WIDTH // SSD_GROUPS
MAIN_WIDTH = 6656
MLSTM_HEADS = 4
MLSTM_DIM = 384
MLSTM_WIDTH = MLSTM_HEADS * MLSTM_DIM
S5_GROUP = 16
S5_GROUPS = 32
S5_WIDTH = S5_GROUP * S5_GROUPS
S5_STATE = 64
S5_LANES = S5_GROUPS * S5_STATE
MOE_GROUPS = 4
EXPERTS_PER_GROUP = 8
N_EXPERTS = MOE_GROUPS * EXPERTS_PER_GROUP
MOE_TOP_K = 2
EXPERT_FF = 512
MOE_ROWS = 128
COMBINE_ROWS = 128

VMEM_LIMIT = 56 * 1024 * 1024


def _cparams(sem):
    return pltpu.CompilerParams(dimension_semantics=sem, vmem_limit_bytes=VMEM_LIMIT)


def _dot(a, b):
    return jnp.dot(a, b, preferred_element_type=F32)


def _dot_nt(a, b):
    return lax.dot_general(a, b, (((1,), (1,)), ((), ())), preferred_element_type=F32)


def _dot_tn(a, b):
    return lax.dot_general(a, b, (((0,), (0,)), ((), ())), preferred_element_type=F32)


def _dot_exact(a, b):
    return jnp.dot(a, b, preferred_element_type=F32, precision=HIGHEST)


def _sigmoid(x):
    return 1.0 / (1.0 + jnp.exp(-x))


def _silu(x):
    return x * _sigmoid(x)


def _softplus(x):
    return jnp.maximum(x, 0.0) + jnp.log1p(jnp.exp(-jnp.abs(x)))


def _tril_f32(n):
    r = lax.broadcasted_iota(jnp.int32, (n, n), 0)
    c = lax.broadcasted_iota(jnp.int32, (n, n), 1)
    return (r >= c).astype(F32)


def _norm_matmul_kernel(x_ref, g_ref, w_ref, w2_ref, o_ref, o2_ref, xn_ref):
    @pl.when(pl.program_id(1) == 0)
    def _():
        x = x_ref[...]
        ms = jnp.mean(x * x, axis=-1, keepdims=True)
        xn = (x * lax.rsqrt(ms + EPS) * g_ref[...]).astype(BF16)
        xn_ref[...] = xn
        o2_ref[...] = _dot(xn, w2_ref[...])

    o_ref[...] = _dot(xn_ref[...], w_ref[...])


def _norm_matmul(x, g, w, w2, tm=1024, tn=512):
    t, d = x.shape
    n = w.shape[1]
    tm = min(tm, t)
    return pl.pallas_call(
        _norm_matmul_kernel,
        grid=(t // tm, n // tn),
        in_specs=[
            pl.BlockSpec((tm, d), lambda i, j: (i, 0)),
            pl.BlockSpec((1, d), lambda i, j: (0, 0)),
            pl.BlockSpec((d, tn), lambda i, j: (0, j)),
            pl.BlockSpec((d, LANES), lambda i, j: (0, 0)),
        ],
        out_specs=[
            pl.BlockSpec((tm, tn), lambda i, j: (i, j)),
            pl.BlockSpec((tm, LANES), lambda i, j: (i, 0)),
        ],
        out_shape=[jax.ShapeDtypeStruct((t, n), F32), jax.ShapeDtypeStruct((t, LANES), F32)],
        scratch_shapes=[pltpu.VMEM((tm, d), BF16)],
        compiler_params=_cparams(("parallel", "arbitrary")),
        name="norm_in_proj",
    )(x, g, w, w2)


def _out_proj_kernel(a_ref, b_ref, wa_ref, wb_ref, r_ref, o_ref):
    o_ref[...] = r_ref[...] + _dot(a_ref[...], wa_ref[...]) + _dot(b_ref[...], wb_ref[...])


def _out_proj(a, b, wa, wb, res, tm=1024, tn=1024):
    t, ka = a.shape
    kb = b.shape[1]
    n = wa.shape[1]
    tm = min(tm, t)
    return pl.pallas_call(
        _out_proj_kernel,
        grid=(t // tm, n // tn),
        in_specs=[
            pl.BlockSpec((tm, ka), lambda i, j: (i, 0)),
            pl.BlockSpec((tm, kb), lambda i, j: (i, 0)),
            pl.BlockSpec((ka, tn), lambda i, j: (0, j)),
            pl.BlockSpec((kb, tn), lambda i, j: (0, j)),
            pl.BlockSpec((tm, tn), lambda i, j: (i, j)),
        ],
        out_specs=pl.BlockSpec((tm, tn), lambda i, j: (i, j)),
        out_shape=jax.ShapeDtypeStruct((t, n), F32),
        compiler_params=_cparams(("parallel", "arbitrary")),
        name="out_proj",
    )(a, b, wa, wb, res)


_RET_LOG_GAMMA = [math.log1p(-(2.0 ** (-5.0 - h))) for h in range(RET_HEADS)]


def _rope_tables(seq):
    inv = 1.0 / (ROPE_BASE ** (np.arange(0, RET_DIM, 2, dtype=np.float64) / RET_DIM))
    ang = np.arange(seq, dtype=np.float64)[:, None] * inv[None, :]
    cos = np.cos(ang)
    sin = np.sin(ang)
    cos2 = np.concatenate([cos, cos], axis=1).astype(np.float32)
    sin2 = np.concatenate([-sin, sin], axis=1).astype(np.float32)
    return jnp.asarray(cos2), jnp.asarray(sin2)


def _head_layer_norm(t, g):
    mu = jnp.mean(t, axis=-1, keepdims=True)
    tc = t - mu
    var = jnp.mean(tc * tc, axis=-1, keepdims=True)
    return tc * lax.rsqrt(var + EPS) * g


def _retention_kernel(q_ref, k_ref, v_ref, g_ref, cos_ref, sin_ref, ng_ref, o_ref, st_ref):
    @pl.when(pl.program_id(1) == 0)
    def _():
        st_ref[...] = jnp.zeros_like(st_ref)

    cos = cos_ref[...]
    sin = sin_ref[...]
    row = lax.broadcasted_iota(jnp.int32, (CHUNK, CHUNK), 0)
    col = lax.broadcasted_iota(jnp.int32, (CHUNK, CHUNK), 1)
    rel = (row - col).astype(F32)
    causal = row >= col
    pos = lax.broadcasted_iota(jnp.int32, (CHUNK, 1), 0).astype(F32)
    scale = RET_DIM ** -0.5
    for h in range(RET_HEADS):
        lg = _RET_LOG_GAMMA[h]
        sl = slice(h * RET_DIM, (h + 1) * RET_DIM)
        q = q_ref[:, sl]
        k = k_ref[:, sl]
        q = q * cos + pltpu.roll(q, RET_DIM // 2, 1) * sin
        k = (k * cos + pltpu.roll(k, RET_DIM // 2, 1) * sin) * scale
        qb = q.astype(BF16)
        kb = k.astype(BF16)
        vb = v_ref[:, sl].astype(BF16)
        decay = jnp.where(causal, jnp.exp(lg * jnp.maximum(rel, 0.0)), 0.0)
        scores = _dot_nt(qb, kb) * decay
        state = st_ref[h]
        out = _dot(scores.astype(BF16), vb)
        out = out + _dot(qb, state.astype(BF16)) * jnp.exp(lg * (pos + 1.0))
        kd = (k * jnp.exp(lg * (CHUNK - 1.0 - pos))).astype(BF16)
        st_ref[h] = math.exp(lg * CHUNK) * state + _dot_tn(kd, vb)
        y = _head_layer_norm(out, ng_ref[h:h + 1, :]) * _silu(g_ref[:, sl])
        o_ref[:, sl] = y.astype(o_ref.dtype)


def _retention(proj, norm_g, cos2, sin2):
    b, l, _ = proj.shape
    col_spec = lambda j: pl.BlockSpec((None, CHUNK, RET_WIDTH), lambda bi, c: (bi, c, j))
    tab_spec = pl.BlockSpec((CHUNK, RET_DIM), lambda bi, c: (c, 0))
    return pl.pallas_call(
        _retention_kernel,
        grid=(b, l // CHUNK),
        in_specs=[col_spec(0), col_spec(1), col_spec(2), col_spec(3), tab_spec, tab_spec,
                  pl.BlockSpec((RET_HEADS, RET_DIM), lambda bi, c: (0, 0))],
        out_specs=pl.BlockSpec((None, CHUNK, RET_WIDTH), lambda bi, c: (bi, c, 0)),
        out_shape=jax.ShapeDtypeStruct((b, l, RET_WIDTH), BF16),
        scratch_shapes=[pltpu.VMEM((RET_HEADS, RET_DIM, RET_DIM), F32)],
        compiler_params=_cparams(("parallel", "arbitrary")),
        name="retention",
    )(proj, proj, proj, proj, cos2, sin2, norm_g)


def _conv_silu(x, buf_ref, w_ref, b_ref, first):
    n = x.shape[0]

    @pl.when(first)
    def _():
        buf_ref[0:CONV_PAD, :] = jnp.zeros((CONV_PAD, x.shape[1]), F32)

    buf_ref[CONV_PAD:CONV_PAD + n, :] = x
    acc = b_ref[...] + w_ref[CONV_W - 1:CONV_W, :] * x
    for j in range(CONV_W - 1):
        off = CONV_PAD - (CONV_W - 1) + j
        acc = acc + w_ref[j:j + 1, :] * buf_ref[off:off + n, :]
    buf_ref[0:CONV_PAD, :] = x[n - CONV_PAD:, :]
    return _silu(acc)


def _ssd_kernel(z_ref, xs_ref, bc_ref, dt_ref, cwx_ref, cbx_ref, cwb_ref, cbb_ref, dtb_ref, alog_ref,
                dskip_ref, ng_ref, o_ref, st_ref, bufx_ref, bufb_ref):
    first = pl.program_id(1) == 0

    @pl.when(first)
    def _():
        st_ref[...] = jnp.zeros_like(st_ref)

    xs = _conv_silu(xs_ref[...], bufx_ref, cwx_ref, cbx_ref, first)
    bc = _conv_silu(bc_ref[...], bufb_ref, cwb_ref, cbb_ref, first)
    dt = _softplus(dt_ref[...] + dtb_ref[...])
    da = dt * (-jnp.exp(alog_ref[...]))
    a_cum = _dot_exact(_tril_f32(CHUNK), da)
    a_cum_t = a_cum.T
    a_last = a_cum[CHUNK - 1:CHUNK, :]
    e_cum = jnp.exp(a_cum)
    to_end = jnp.exp(a_last - a_cum)
    e_last = jnp.exp(a_last)
    row = lax.broadcasted_iota(jnp.int32, (CHUNK, CHUNK), 0)
    col = lax.broadcasted_iota(jnp.int32, (CHUNK, CHUNK), 1)
    causal = row >= col
    gn = SSD_GROUPS * SSD_STATE
    for g in range(SSD_GROUPS):
        bm = bc[:, g * SSD_STATE:(g + 1) * SSD_STATE].astype(BF16)
        cm = bc[:, gn + g * SSD_STATE:gn + (g + 1) * SSD_STATE].astype(BF16)
        cb = _dot_nt(cm, bm)
        ys = []
        for r in range(SSD_HPG):
            hd = g * SSD_HPG + r
            sl = slice(hd * SSD_HEAD_DIM, (hd + 1) * SSD_HEAD_DIM)
            x_h = xs[:, sl]
            xdt = x_h * dt[:, hd:hd + 1]
            seg = jnp.exp(jnp.where(causal, a_cum[:, hd:hd + 1] - a_cum_t[hd:hd + 1, :], -jnp.inf))
            state = st_ref[hd]
            y = _dot((cb * seg).astype(BF16), xdt.astype(BF16))
            y = y + _dot(cm, state.astype(BF16)) * e_cum[:, hd:hd + 1]
            st_ref[hd] = e_last[:, hd:hd + 1] * state + _dot_tn(bm, (xdt * to_end[:, hd:hd + 1]).astype(BF16))
            ys.append(y + dskip_ref[:, sl] * x_h)
        gsl = slice(g * SSD_GW, (g + 1) * SSD_GW)
        y = jnp.concatenate(ys, axis=1) * _silu(z_ref[:, gsl])
        y = y * lax.rsqrt(jnp.mean(y * y, axis=-1, keepdims=True) + EPS)
        o_ref[:, gsl] = (y * ng_ref[:, gsl]).astype(o_ref.dtype)


def _ssd(proj, gates, conv_w, conv_b, dt_bias, a_log, d_skip, norm_g):
    b, l, _ = proj.shape
    bcw = 2 * SSD_GROUPS * SSD_STATE
    pad = lambda v: jnp.pad(v.astype(F32), (0, LANES - v.shape[0]))[None, :]
    full = lambda shape: pl.BlockSpec(shape, lambda bi, c: (0,) * len(shape))
    z_off = 4 * RET_WIDTH // SSD_WIDTH
    return pl.pallas_call(
        _ssd_kernel,
        grid=(b, l // CHUNK),
        in_specs=[
            pl.BlockSpec((None, CHUNK, SSD_WIDTH), lambda bi, c: (bi, c, z_off)),
            pl.BlockSpec((None, CHUNK, SSD_WIDTH), lambda bi, c: (bi, c, z_off + 1)),
            pl.BlockSpec((None, CHUNK, bcw), lambda bi, c: (bi, c, (z_off + 2) * SSD_WIDTH // bcw)),
            pl.BlockSpec((None, CHUNK, LANES), lambda bi, c: (bi, c, 0)),
            full((CONV_W, SSD_WIDTH)), full((1, SSD_WIDTH)), full((CONV_W, bcw)), full((1, bcw)),
            full((1, LANES)), full((1, LANES)), full((1, SSD_WIDTH)), full((1, SSD_WIDTH)),
        ],
        out_specs=pl.BlockSpec((None, CHUNK, SSD_WIDTH), lambda bi, c: (bi, c, 0)),
        out_shape=jax.ShapeDtypeStruct((b, l, SSD_WIDTH), BF16),
        scratch_shapes=[
            pltpu.VMEM((SSD_HEADS, SSD_STATE, SSD_HEAD_DIM), F32),
            pltpu.VMEM((CONV_PAD + CHUNK, SSD_WIDTH), F32),
            pltpu.VMEM((CONV_PAD + CHUNK, bcw), F32),
        ],
        compiler_params=_cparams(("parallel", "arbitrary")),
        name="ssd",
    )(proj, proj, proj, gates,
      conv_w[:, :SSD_WIDTH], conv_b[None, :SSD_WIDTH], conv_w[:, SSD_WIDTH:], conv_b[None, SSD_WIDTH:],
      pad(dt_bias), pad(a_log), jnp.repeat(d_skip.astype(F32), SSD_HEAD_DIM)[None, :], norm_g[None, :])


def _mlstm_kernel(q_ref, k_ref, v_ref, o_ref, gt_ref, cwq_ref, cbq_ref, cwk_ref, cbk_ref, gb_ref, ng_ref,
                  out_ref, c_ref, n_ref, m_ref, bufq_ref, bufk_ref):
    hd = pl.program_id(1)
    first = pl.program_id(2) == 0

    @pl.when(first)
    def _():
        c_ref[...] = jnp.zeros_like(c_ref)
        n_ref[...] = jnp.zeros_like(n_ref)
        m_ref[...] = jnp.full(m_ref.shape, -jnp.inf, F32)

    q = _conv_silu(q_ref[...], bufq_ref, cwq_ref, cbq_ref, first) * (MLSTM_DIM ** -0.5)
    k = _conv_silu(k_ref[...], bufk_ref, cwk_ref, cbk_ref, first)
    qb = q.astype(BF16)
    kb = k.astype(BF16)
    vb = v_ref[...].astype(BF16)

    pre = gt_ref[...] + gb_ref[...]
    lane = lax.broadcasted_iota(jnp.int32, (CHUNK, LANES), 1)
    lf_all = -_softplus(-pre)
    ig = jnp.sum(jnp.where(lane == hd, pre, 0.0), axis=1, keepdims=True)
    lf = jnp.sum(jnp.where(lane == hd + MLSTM_HEADS, lf_all, 0.0), axis=1, keepdims=True)
    bcum = _dot_exact(_tril_f32(CHUNK), jnp.broadcast_to(lf, (CHUNK, LANES)))
    ig_t = jnp.broadcast_to(ig, (CHUNK, LANES)).T[0:1, :]
    bcum_t = bcum.T[0:1, :]
    bcum = bcum[:, 0:1]

    row = lax.broadcasted_iota(jnp.int32, (CHUNK, CHUNK), 0)
    col = lax.broadcasted_iota(jnp.int32, (CHUNK, CHUNK), 1)
    log_d = jnp.where(row >= col, bcum - bcum_t + ig_t, -jnp.inf)
    m_prev = m_ref[...]
    m_inter = bcum + m_prev
    m_t = jnp.maximum(jnp.max(log_d, axis=1, keepdims=True), m_inter)
    scores = _dot_nt(qb, kb) * jnp.exp(log_d - m_t)
    inter_scale = jnp.exp(m_inter - m_t)
    c_st = c_ref[...]
    n_st = n_ref[...]
    num = _dot(scores.astype(BF16), vb) + inter_scale * _dot(qb, c_st.astype(BF16))
    den = jnp.sum(scores, axis=1, keepdims=True) + inter_scale * jnp.sum(q * n_st, axis=1, keepdims=True)
    hval = num / jnp.maximum(jnp.abs(den), jnp.exp(-m_t))

    b_last = bcum[CHUNK - 1:CHUNK, :]
    w_log = b_last - bcum + ig
    m_new = jnp.maximum(b_last + m_prev, jnp.max(w_log, axis=0, keepdims=True))
    kw = k * jnp.exp(w_log - m_new)
    prev_scale = jnp.exp(b_last + m_prev - m_new)
    c_ref[...] = prev_scale * c_st + _dot_tn(kw.astype(BF16), vb)
    n_ref[...] = prev_scale * n_st + jnp.sum(kw, axis=0, keepdims=True)
    m_ref[...] = m_new

    y = _head_layer_norm(hval, ng_ref[...]) * _sigmoid(o_ref[...])
    out_ref[...] = y.astype(out_ref.dtype)


def _mlstm(proj, gates, conv_w, conv_b, i_bias, f_bias, norm_g):
    b, l, _ = proj.shape
    nh = MLSTM_HEADS
    head_spec = lambda j: pl.BlockSpec((None, CHUNK, MLSTM_DIM), lambda bi, h, c: (bi, c, j * nh + h))
    cw_spec = lambda j: pl.BlockSpec((CONV_W, MLSTM_DIM), lambda bi, h, c: (0, j * nh + h))
    cb_spec = lambda j: pl.BlockSpec((1, MLSTM_DIM), lambda bi, h, c: (0, j * nh + h))
    gate_bias = jnp.pad(jnp.concatenate([i_bias, f_bias]).astype(F32), (0, LANES - 2 * nh))[None, :]
    return pl.pallas_call(
        _mlstm_kernel,
        grid=(b, nh, l // CHUNK),
        in_specs=[
            head_spec(0), head_spec(1), head_spec(2), head_spec(3),
            pl.BlockSpec((None, CHUNK, LANES), lambda bi, h, c: (bi, c, 0)),
            cw_spec(0), cb_spec(0), cw_spec(1), cb_spec(1),
            pl.BlockSpec((1, LANES), lambda bi, h, c: (0, 0)),
            pl.BlockSpec((1, MLSTM_DIM), lambda bi, h, c: (0, h)),
        ],
        out_specs=pl.BlockSpec((None, CHUNK, MLSTM_DIM), lambda bi, h, c: (bi, c, h)),
        out_shape=jax.ShapeDtypeStruct((b, l, MLSTM_WIDTH), BF16),
        scratch_shapes=[
            pltpu.VMEM((MLSTM_DIM, MLSTM_DIM), F32),
            pltpu.VMEM((1, MLSTM_DIM), F32),
            pltpu.VMEM((1, 1), F32),
            pltpu.VMEM((CONV_PAD + CHUNK, MLSTM_DIM), F32),
            pltpu.VMEM((CONV_PAD + CHUNK, MLSTM_DIM), F32),
        ],
        compiler_params=_cparams(("parallel", "parallel", "arbitrary")),
        name="mlstm",
    )(proj, proj, proj, proj, gates, conv_w, conv_b[None, :], conv_w, conv_b[None, :], gate_bias,
      norm_g.reshape(1, MLSTM_WIDTH))


_S5_LOG_STEPS = int(math.log2(CHUNK))


def _shift_rows(x, d):
    n = x.shape[0]
    if d % CONV_PAD == 0:
        return jnp.concatenate([jnp.zeros((d, x.shape[1]), x.dtype), x[:n - d, :]], axis=0)
    rolled = pltpu.roll(x, d, 0)
    rows = lax.broadcasted_iota(jnp.int32, x.shape, 0)
    return jnp.where(rows >= d, rolled, 0.0)


def _s5_kernel(u_ref, bre_ref, bim_ref, pw_re_ref, pw_im_ref, car_re_ref, car_im_ref, cre_ref, cim_ref,
               dskip_ref, wglu_ref, bglu_ref, o_ref, sre_ref, sim_ref, st_re_ref, st_im_ref):
    @pl.when(pl.program_id(1) == 0)
    def _():
        st_re_ref[...] = jnp.zeros_like(st_re_ref)
        st_im_ref[...] = jnp.zeros_like(st_im_ref)

    u = u_ref[...]
    ub = u.astype(BF16)
    for j in range(S5_LANES // LANES):
        sl = slice(j * LANES, (j + 1) * LANES)
        x_re = _dot(ub, bre_ref[:, sl])
        x_im = _dot(ub, bim_ref[:, sl])
        for s in range(_S5_LOG_STEPS):
            a_re = pw_re_ref[s:s + 1, sl]
            a_im = pw_im_ref[s:s + 1, sl]
            sh_re = _shift_rows(x_re, 1 << s)
            sh_im = _shift_rows(x_im, 1 << s)
            x_re, x_im = (x_re + (a_re * sh_re - a_im * sh_im), x_im + (a_re * sh_im + a_im * sh_re))
        p_re = st_re_ref[:, sl]
        p_im = st_im_ref[:, sl]
        c_re = car_re_ref[:, sl]
        c_im = car_im_ref[:, sl]
        x_re, x_im = (x_re + (c_re * p_re - c_im * p_im), x_im + (c_re * p_im + c_im * p_re))
        st_re_ref[:, sl] = x_re[CHUNK - 1:CHUNK, :]
        st_im_ref[:, sl] = x_im[CHUNK - 1:CHUNK, :]
        sre_ref[:, sl] = x_re.astype(BF16)
        sim_ref[:, sl] = x_im.astype(BF16)

    y = _dot(sre_ref[...], cre_ref[...]) - _dot(sim_ref[...], cim_ref[...]) + dskip_ref[...] * u
    y = 0.5 * y * (1.0 + jnp.tanh(math.sqrt(2.0 / math.pi) * (y + 0.044715 * (y * y * y))))
    vg = _dot(y.astype(BF16), wglu_ref[...]) + bglu_ref[...]
    o_ref[...] = (vg[:, :S5_WIDTH] * _sigmoid(vg[:, S5_WIDTH:])).astype(o_ref.dtype)


def _s5_params(a_re, a_im, log_dt, b_re, b_im, c_re, c_im):
    a_re = a_re.astype(F32)
    a_im = a_im.astype(F32)
    dt = jnp.exp(log_dt.astype(F32))[:, None]
    mag = jnp.exp(a_re * dt)
    abar_re = mag * jnp.cos(a_im * dt)
    abar_im = mag * jnp.sin(a_im * dt)
    den = a_re * a_re + a_im * a_im
    nr = abar_re - 1.0
    coef_re = (nr * a_re + abar_im * a_im) / den
    coef_im = (abar_im * a_re - nr * a_im) / den
    b_re = b_re.astype(F32)
    b_im = b_im.astype(F32)
    bbar_re = coef_re[..., None] * b_re - coef_im[..., None] * b_im
    bbar_im = coef_re[..., None] * b_im + coef_im[..., None] * b_re
    eye = jnp.eye(S5_GROUPS, dtype=F32)
    in_map = lambda m: jnp.einsum('gnc,gh->gchn', m, eye).reshape(S5_WIDTH, S5_LANES).astype(BF16)
    out_map = lambda m: jnp.einsum('gcn,gh->gnhc', m.astype(F32), eye).reshape(S5_LANES, S5_WIDTH).astype(BF16)

    def powers(exps):
        e = exps[:, None, None]
        m = jnp.exp(e * (a_re * dt)[None])
        ang = e * (a_im * dt)[None]
        return (m * jnp.cos(ang)).reshape(-1, S5_LANES), (m * jnp.sin(ang)).reshape(-1, S5_LANES)

    pw_re, pw_im = powers(jnp.asarray([float(1 << s) for s in range(_S5_LOG_STEPS)], F32))
    car_re, car_im = powers(jnp.arange(1, CHUNK + 1, dtype=F32))
    pad8 = lambda m: jnp.pad(m, ((0, CONV_PAD - m.shape[0]), (0, 0)))
    return (in_map(bbar_re), in_map(bbar_im), pad8(pw_re), pad8(pw_im), car_re, car_im,
            out_map(c_re), out_map(c_im))


def _s5(proj, a_re, a_im, log_dt, b_re, b_im, c_re, c_im, d_skip, w_glu, b_glu):
    b, l, _ = proj.shape
    tabs = _s5_params(a_re, a_im, log_dt, b_re, b_im, c_re, c_im)
    full = lambda shape: pl.BlockSpec(shape, lambda bi, c: (0,) * len(shape))
    u_off = 4 * MLSTM_WIDTH // S5_WIDTH
    return pl.pallas_call(
        _s5_kernel,
        grid=(b, l // CHUNK),
        in_specs=[
            pl.BlockSpec((None, CHUNK, S5_WIDTH), lambda bi, c: (bi, c, u_off)),
            full((S5_WIDTH, S5_LANES)), full((S5_WIDTH, S5_LANES)),
            full((CONV_PAD, S5_LANES)), full((CONV_PAD, S5_LANES)),
            full((CHUNK, S5_LANES)), full((CHUNK, S5_LANES)),
            full((S5_LANES, S5_WIDTH)), full((S5_LANES, S5_WIDTH)),
            full((1, S5_WIDTH)), full((S5_WIDTH, 2 * S5_WIDTH)), full((1, 2 * S5_WIDTH)),
        ],
        out_specs=pl.BlockSpec((None, CHUNK, S5_WIDTH), lambda bi, c: (bi, c, 0)),
        out_shape=jax.ShapeDtypeStruct((b, l, S5_WIDTH), BF16),
        scratch_shapes=[
            pltpu.VMEM((CHUNK, S5_LANES), BF16), pltpu.VMEM((CHUNK, S5_LANES), BF16),
            pltpu.VMEM((1, S5_LANES), F32), pltpu.VMEM((1, S5_LANES), F32),
        ],
        compiler_params=_cparams(("parallel", "arbitrary")),
        name="s5",
    )(proj, *tabs, d_skip.astype(F32).reshape(1, S5_WIDTH), w_glu.astype(BF16), b_glu.astype(F32)[None, :])


def _router_kernel(x_ref, g_ref, w_ref, b_ref, ids_ref, wts_ref):
    x = x_ref[...]
    xn = x * lax.rsqrt(jnp.mean(x * x, axis=-1, keepdims=True) + EPS) * g_ref[...]
    logits = _dot_exact(xn, w_ref[...]) + b_ref[...]
    lane = lax.broadcasted_iota(jnp.int32, logits.shape, 1)
    big = jnp.int32(LANES)
    neg = -jnp.inf

    def top(vals):
        m = jnp.max(vals, axis=1, keepdims=True)
        idx = jnp.min(jnp.where(vals == m, lane, big), axis=1, keepdims=True)
        return m, idx

    gl = jnp.where(lane < MOE_GROUPS, logits, neg)
    g_max, g_idx = top(gl)
    g_w = 1.0 / jnp.sum(jnp.exp(gl - g_max), axis=1, keepdims=True)
    lo = MOE_GROUPS + EXPERTS_PER_GROUP * g_idx
    el = jnp.where((lane >= lo) & (lane < lo + EXPERTS_PER_GROUP), logits, neg)
    m1, j1 = top(el)
    m2, j2 = top(jnp.where(lane == j1, neg, el))
    e2 = jnp.exp(m2 - m1)
    w1 = g_w / (1.0 + e2)
    w2 = g_w * e2 / (1.0 + e2)
    ids_ref[...] = jnp.where(lane == 0, j1 - MOE_GROUPS, jnp.where(lane == 1, j2 - MOE_GROUPS, 0))
    wts_ref[...] = jnp.where(lane == 0, w1, jnp.where(lane == 1, w2, 0.0))


def _router(x, g, w, bias, tm=512):
    t, d = x.shape
    return pl.pallas_call(
        _router_kernel,
        grid=(t // tm,),
        in_specs=[
            pl.BlockSpec((tm, d), lambda i: (i, 0)),
            pl.BlockSpec((1, d), lambda i: (0, 0)),
            pl.BlockSpec((d, LANES), lambda i: (0, 0)),
            pl.BlockSpec((1, LANES), lambda i: (0, 0)),
        ],
        out_specs=[pl.BlockSpec((tm, LANES), lambda i: (i, 0)), pl.BlockSpec((tm, LANES), lambda i: (i, 0))],
        out_shape=[jax.ShapeDtypeStruct((t, LANES), jnp.int32), jax.ShapeDtypeStruct((t, LANES), F32)],
        compiler_params=_cparams(("parallel",)),
        name="moe_router",
    )(x, g, w, bias)


def _row_gather_copy(src_hbm, idx_ref, base, buf_ref, slot, sem_ref, r):
    return pltpu.make_async_copy(src_hbm.at[pl.ds(idx_ref[base + r], 1)], buf_ref.at[slot, pl.ds(r, 1)],
                                 sem_ref.at[slot])


def _start_row_gather(src_hbm, idx_ref, base, buf_ref, slot, sem_ref, rows):
    def body(r, carry):
        _row_gather_copy(src_hbm, idx_ref, base, buf_ref, slot, sem_ref, r).start()
        return carry

    lax.fori_loop(0, rows, body, 0)


def _wait_row_gather(src_hbm, buf_ref, slot, sem_ref, rows):
    pltpu.make_async_copy(src_hbm.at[pl.ds(0, rows)], buf_ref.at[slot], sem_ref.at[slot]).wait()


def _expert_kernel(be_ref, src_ref, nused_ref, x_hbm, g_ref, rw_ref, wg_ref, wu_ref, wd_ref, o_ref,
                   xbuf_ref, sem_ref, wgb_ref, wub_ref, wdb_ref):
    i = pl.program_id(0)
    n_used = nused_ref[0]
    slot = lax.rem(i, 2)

    @pl.when(i == 0)
    def _():
        _start_row_gather(x_hbm, src_ref, 0, xbuf_ref, 0, sem_ref, MOE_ROWS)

    @pl.when(i + 1 < n_used)
    def _():
        _start_row_gather(x_hbm, src_ref, (i + 1) * MOE_ROWS, xbuf_ref, 1 - slot, sem_ref, MOE_ROWS)

    @pl.when((i == 0) | (be_ref[i] != be_ref[jnp.maximum(i - 1, 0)]))
    def _():
        wgb_ref[...] = wg_ref[...].astype(BF16)
        wub_ref[...] = wu_ref[...].astype(BF16)
        wdb_ref[...] = wd_ref[...].astype(BF16)

    @pl.when(i < n_used)
    def _():
        _wait_row_gather(x_hbm, xbuf_ref, slot, sem_ref, MOE_ROWS)
        x = xbuf_ref[slot]
        xn = (x * lax.rsqrt(jnp.mean(x * x, axis=-1, keepdims=True) + EPS) * g_ref[...]).astype(BF16)
        act = (_silu(_dot(xn, wgb_ref[...])) * _dot(xn, wub_ref[...])).astype(BF16)
        o_ref[...] = _dot(act, wdb_ref[...]) * rw_ref[...]

    @pl.when(i >= n_used)
    def _():
        o_ref[...] = jnp.zeros_like(o_ref)


def _expert_ffn(x, g, block_expert, src_tok, n_used, row_w, w_gate, w_up, w_down):
    t, d = x.shape
    n_rows = src_tok.shape[0]
    n_blocks = n_rows // MOE_ROWS
    ff = w_gate.shape[-1]
    grid_spec = pltpu.PrefetchScalarGridSpec(
        num_scalar_prefetch=3,
        grid=(n_blocks,),
        in_specs=[
            pl.BlockSpec(memory_space=pl.ANY),
            pl.BlockSpec((1, d), lambda i, be, src, nu: (0, 0)),
            pl.BlockSpec((MOE_ROWS, 1), lambda i, be, src, nu: (i, 0)),
            pl.BlockSpec((None, d, ff), lambda i, be, src, nu: (be[i], 0, 0)),
            pl.BlockSpec((None, d, ff), lambda i, be, src, nu: (be[i], 0, 0)),
            pl.BlockSpec((None, ff, d), lambda i, be, src, nu: (be[i], 0, 0)),
        ],
        out_specs=pl.BlockSpec((MOE_ROWS, d), lambda i, be, src, nu: (i, 0)),
        scratch_shapes=[
            pltpu.VMEM((2, MOE_ROWS, d), F32),
            pltpu.SemaphoreType.DMA((2,)),
            pltpu.VMEM((d, ff), BF16), pltpu.VMEM((d, ff), BF16), pltpu.VMEM((ff, d), BF16),
        ],
    )
    return pl.pallas_call(
        _expert_kernel,
        grid_spec=grid_spec,
        out_shape=jax.ShapeDtypeStruct((n_rows, d), F32),
        compiler_params=_cparams(("arbitrary",)),
        name="moe_experts",
    )(block_expert, src_tok, n_used, x, g, row_w, w_gate, w_up, w_down)


def _combine_kernel(d0_ref, d1_ref, y_hbm, h_ref, fg_ref, o_ref, buf0_ref, buf1_ref, sem0_ref, sem1_ref, *,
                    final_norm):
    i = pl.program_id(0)
    n = pl.num_programs(0)
    slot = lax.rem(i, 2)

    def start(step, sl):
        _start_row_gather(y_hbm, d0_ref, step * COMBINE_ROWS, buf0_ref, sl, sem0_ref, COMBINE_ROWS)
        _start_row_gather(y_hbm, d1_ref, step * COMBINE_ROWS, buf1_ref, sl, sem1_ref, COMBINE_ROWS)

    @pl.when(i == 0)
    def _():
        start(0, 0)

    @pl.when(i + 1 < n)
    def _():
        start(i + 1, 1 - slot)

    _wait_row_gather(y_hbm, buf0_ref, slot, sem0_ref, COMBINE_ROWS)
    _wait_row_gather(y_hbm, buf1_ref, slot, sem1_ref, COMBINE_ROWS)
    y = h_ref[...] + (buf0_ref[slot] + buf1_ref[slot])
    if final_norm:
        y = y * lax.rsqrt(jnp.mean(y * y, axis=-1, keepdims=True) + EPS) * fg_ref[...]
    o_ref[...] = y


def _combine(y_rows, dest0, dest1, h, final_g, final_norm):
    t, d = h.shape
    grid_spec = pltpu.PrefetchScalarGridSpec(
        num_scalar_prefetch=2,
        grid=(t // COMBINE_ROWS,),
        in_specs=[
            pl.BlockSpec(memory_space=pl.ANY),
            pl.BlockSpec((COMBINE_ROWS, d), lambda i, d0, d1: (i, 0)),
            pl.BlockSpec((1, d), lambda i, d0, d1: (0, 0)),
        ],
        out_specs=pl.BlockSpec((COMBINE_ROWS, d), lambda i, d0, d1: (i, 0)),
        scratch_shapes=[
            pltpu.VMEM((2, COMBINE_ROWS, d), F32), pltpu.VMEM((2, COMBINE_ROWS, d), F32),
            pltpu.SemaphoreType.DMA((2,)), pltpu.SemaphoreType.DMA((2,)),
        ],
    )
    return pl.pallas_call(
        functools.partial(_combine_kernel, final_norm=final_norm),
        grid_spec=grid_spec,
        out_shape=jax.ShapeDtypeStruct((t, d), F32),
        compiler_params=_cparams(("arbitrary",)),
        name="moe_combine",
    )(dest0, dest1, y_rows, h, final_g)


def _moe(h, norm_g, router_g, router_g_b, router_e, router_e_b, w_gate, w_up, w_down, final_g, final_norm):
    t, d = h.shape
    g = norm_g.astype(F32)[None, :]
    pad_cols = LANES - MOE_GROUPS - N_EXPERTS
    w_r = jnp.pad(jnp.concatenate([router_g, router_e], axis=1).astype(F32), ((0, 0), (0, pad_cols)))
    b_r = jnp.pad(jnp.concatenate([router_g_b, router_e_b]).astype(F32), (0, pad_cols))[None, :]
    ids, wts = _router(h, g, w_r, b_r)

    n_slots = t * MOE_TOP_K
    e_flat = ids[:, :MOE_TOP_K].reshape(n_slots)
    onehot = (e_flat[:, None] == jnp.arange(N_EXPERTS, dtype=jnp.int32)[None, :]).astype(jnp.int32)
    csum = jnp.cumsum(onehot, axis=0)
    counts = csum[-1]
    rank = jnp.take_along_axis(csum, e_flat[:, None], axis=1)[:, 0] - 1
    padded = (counts + MOE_ROWS - 1) // MOE_ROWS * MOE_ROWS
    pad_end = jnp.cumsum(padded)
    dest = (pad_end - padded)[e_flat] + rank
    n_blocks = (n_slots + N_EXPERTS * (MOE_ROWS - 1) + MOE_ROWS - 1) // MOE_ROWS
    n_rows = n_blocks * MOE_ROWS
    tok = jnp.arange(n_slots, dtype=jnp.int32) // MOE_TOP_K
    src_tok = jnp.zeros((n_rows,), jnp.int32).at[dest].set(tok)
    row_w = jnp.zeros((n_rows,), F32).at[dest].set(wts[:, :MOE_TOP_K].reshape(n_slots))
    block_expert = jnp.minimum(
        jnp.searchsorted(pad_end, jnp.arange(n_blocks, dtype=jnp.int32) * MOE_ROWS, side='right'),
        N_EXPERTS - 1).astype(jnp.int32)
    n_used = (pad_end[-1:] // MOE_ROWS).astype(jnp.int32)

    y_rows = _expert_ffn(h, g, block_expert, src_tok, n_used, row_w[:, None], w_gate, w_up, w_down)
    dest2 = dest.reshape(t, MOE_TOP_K).astype(jnp.int32)
    return _combine(y_rows, dest2[:, 0], dest2[:, 1], h, final_g.astype(F32)[None, :], final_norm)


def _split_in_proj(w_in, gate_lo, gate_hi):
    main = jnp.concatenate([w_in[:, :gate_lo], w_in[:, gate_hi:]], axis=1).astype(BF16)
    small = jnp.pad(w_in[:, gate_lo:gate_hi], ((0, 0), (0, LANES - (gate_hi - gate_lo)))).astype(BF16)
    return main, small


def _even_layer(h, b, l, norm_g, w_in, ret_norm, conv_w, conv_b, dt_bias, a_log, d_skip, ssd_norm, w_out):
    w_main, w_small = _split_in_proj(w_in, MAIN_WIDTH, MAIN_WIDTH + SSD_HEADS)
    proj, gates = _norm_matmul(h, norm_g.astype(F32)[None, :], w_main, w_small)
    proj = proj.reshape(b, l, MAIN_WIDTH)
    gates = gates.reshape(b, l, LANES)
    cos2, sin2 = _rope_tables(l)
    a_out = _retention(proj, ret_norm.astype(F32), cos2, sin2)
    b_out = _ssd(proj, gates, conv_w.astype(F32), conv_b.astype(F32), dt_bias, a_log, d_skip, ssd_norm.astype(F32))
    w_out = w_out.astype(BF16)
    return _out_proj(a_out.reshape(b * l, RET_WIDTH), b_out.reshape(b * l, SSD_WIDTH),
                     w_out[:RET_WIDTH], w_out[RET_WIDTH:], h)


def _odd_layer(h, b, l, norm_g, w_in, conv_w, conv_b, i_bias, f_bias, mlstm_norm, a_re, a_im, log_dt, b_re, b_im,
               c_re, c_im, d_skip, w_glu, b_glu, w_out):
    gate_lo = 4 * MLSTM_WIDTH
    w_main, w_small = _split_in_proj(w_in, gate_lo, gate_lo + 2 * MLSTM_HEADS)
    proj, gates = _norm_matmul(h, norm_g.astype(F32)[None, :], w_main, w_small)
    proj = proj.reshape(b, l, MAIN_WIDTH)
    gates = gates.reshape(b, l, LANES)
    c_out = _mlstm(proj, gates, conv_w.astype(F32), conv_b.astype(F32), i_bias, f_bias, mlstm_norm.astype(F32))
    d_out = _s5(proj, a_re, a_im, log_dt, b_re, b_im, c_re, c_im, d_skip, w_glu, b_glu)
    w_out = w_out.astype(BF16)
    return _out_proj(c_out.reshape(b * l, MLSTM_WIDTH), d_out.reshape(b * l, S5_WIDTH),
                     w_out[:MLSTM_WIDTH], w_out[MLSTM_WIDTH:], h)


def kernel(x, even_norm, even_w_in, ret_norm, ssd_conv_w, ssd_conv_b, ssd_dt_bias, ssd_a_log, ssd_d, ssd_norm, even_w_out, odd_norm, odd_w_in, mlstm_conv_w, mlstm_conv_b, mlstm_i_bias, mlstm_f_bias, mlstm_norm, s5_a_re, s5_a_im, s5_log_dt, s5_b_re, s5_b_im, s5_c_re, s5_c_im, s5_d, s5_w_glu, s5_b_glu, odd_w_out, moe_norm, moe_router_g, moe_router_g_b, moe_router_e, moe_router_e_b, moe_w_gate, moe_w_up, moe_w_down, final_norm):
    b, l, d = x.shape
    depth = moe_norm.shape[0]
    h = x.reshape(b * l, d)
    for layer in range(depth):
        i = layer // 2
        if layer % 2 == 0:
            h = _even_layer(h, b, l, even_norm[i], even_w_in[i], ret_norm[i], ssd_conv_w[i], ssd_conv_b[i],
                            ssd_dt_bias[i], ssd_a_log[i], ssd_d[i], ssd_norm[i], even_w_out[i])
        else:
            h = _odd_layer(h, b, l, odd_norm[i], odd_w_in[i], mlstm_conv_w[i], mlstm_conv_b[i], mlstm_i_bias[i],
                           mlstm_f_bias[i], mlstm_norm[i], s5_a_re[i], s5_a_im[i], s5_log_dt[i], s5_b_re[i],
                           s5_b_im[i], s5_c_re[i], s5_c_im[i], s5_d[i], s5_w_glu[i], s5_b_glu[i], odd_w_out[i])
        h = _moe(h, moe_norm[layer], moe_router_g[layer], moe_router_g_b[layer], moe_router_e[layer],
                 moe_router_e_b[layer], moe_w_gate[layer], moe_w_up[layer], moe_w_down[layer], final_norm,
                 final_norm=(layer == depth - 1))
    return h.reshape(b, l, d)
```

```python
import functools
import math

import numpy as np
import jax
import jax.numpy as jnp
from jax import lax
from jax.experimental import pallas as pl
from jax.experimental.pallas import tpu as pltpu

F32 = jnp.float32
BF16 = jnp.bfloat16
HIGHEST = lax.Precision.HIGHEST

D_MODEL = 2048
CHUNK = 128
CONV_W = 4
EPS = 1e-6
LANES = 128
CONV_PAD = 8

RET_HEADS = 8
RET_DIM = 128
RET_WIDTH = RET_HEADS * RET_DIM
ROPE_BASE = 10000.0
SSD_HEAD_DIM = 64
SSD_HEADS = 16
SSD_WIDTH = SSD_HEADS * SSD_HEAD_DIM
SSD_GROUPS = 2
SSD_HPG = SSD_HEADS // SSD_GROUPS
SSD_STATE = 128
SSD_GW = SSD_WIDTH // SSD_GROUPS
MAIN_WIDTH = 6656
MLSTM_HEADS = 4
MLSTM_DIM = 384
MLSTM_WIDTH = MLSTM_HEADS * MLSTM_DIM
S5_GROUP = 16
S5_GROUPS = 32
S5_WIDTH = S5_GROUP * S5_GROUPS
S5_STATE = 64
S5_LANES = S5_GROUPS * S5_STATE
MOE_GROUPS = 4
EXPERTS_PER_GROUP = 8
N_EXPERTS = MOE_GROUPS * EXPERTS_PER_GROUP
MOE_TOP_K = 2
EXPERT_FF = 512
MOE_ROWS = 256
COMBINE_ROWS = 128

VMEM_LIMIT = 56 * 1024 * 1024


def _cparams(sem):
    return pltpu.CompilerParams(dimension_semantics=sem, vmem_limit_bytes=VMEM_LIMIT)


def _dot(a, b):
    return jnp.dot(a, b, preferred_element_type=F32)


def _dot_nt(a, b):
    return lax.dot_general(a, b, (((1,), (1,)), ((), ())), preferred_element_type=F32)


def _dot_tn(a, b):
    return lax.dot_general(a, b, (((0,), (0,)), ((), ())), preferred_element_type=F32)


def _dot_exact(a, b):
    return jnp.dot(a, b, preferred_element_type=F32, precision=HIGHEST)


def _sigmoid(x):
    return 1.0 / (1.0 + jnp.exp(-x))


def _silu(x):
    return x * _sigmoid(x)


def _softplus(x):
    return jnp.maximum(x, 0.0) + jnp.log1p(jnp.exp(-jnp.abs(x)))


def _tril_f32(n):
    r = lax.broadcasted_iota(jnp.int32, (n, n), 0)
    c = lax.broadcasted_iota(jnp.int32, (n, n), 1)
    return (r >= c).astype(F32)


def _norm_matmul_kernel(x_ref, g_ref, w_ref, w2_ref, o_ref, o2_ref, xn_ref):
    @pl.when(pl.program_id(1) == 0)
    def _():
        x = x_ref[...]
        ms = jnp.mean(x * x, axis=-1, keepdims=True)
        xn = (x * lax.rsqrt(ms + EPS) * g_ref[...]).astype(BF16)
        xn_ref[...] = xn
        o2_ref[...] = _dot(xn, w2_ref[...])

    o_ref[...] = _dot(xn_ref[...], w_ref[...])


def _norm_matmul(x, g, w, w2, tm=1024, tn=512):
    t, d = x.shape
    n = w.shape[1]
    tm = min(tm, t)
    return pl.pallas_call(
        _norm_matmul_kernel,
        grid=(t // tm, n // tn),
        in_specs=[
            pl.BlockSpec((tm, d), lambda i, j: (i, 0)),
            pl.BlockSpec((1, d), lambda i, j: (0, 0)),
            pl.BlockSpec((d, tn), lambda i, j: (0, j)),
            pl.BlockSpec((d, LANES), lambda i, j: (0, 0)),
        ],
        out_specs=[
            pl.BlockSpec((tm, tn), lambda i, j: (i, j)),
            pl.BlockSpec((tm, LANES), lambda i, j: (i, 0)),
        ],
        out_shape=[jax.ShapeDtypeStruct((t, n), F32), jax.ShapeDtypeStruct((t, LANES), F32)],
        scratch_shapes=[pltpu.VMEM((tm, d), BF16)],
        compiler_params=_cparams(("parallel", "arbitrary")),
        name="norm_in_proj",
    )(x, g, w, w2)


def _out_proj_kernel(a_ref, b_ref, wa_ref, wb_ref, r_ref, o_ref):
    o_ref[...] = r_ref[...] + _dot(a_ref[...], wa_ref[...]) + _dot(b_ref[...], wb_ref[...])


def _out_proj(a, b, wa, wb, res, tm=1024, tn=1024):
    t, ka = a.shape
    kb = b.shape[1]
    n = wa.shape[1]
    tm = min(tm, t)
    return pl.pallas_call(
        _out_proj_kernel,
        grid=(t // tm, n // tn),
        in_specs=[
            pl.BlockSpec((tm, ka), lambda i, j: (i, 0)),
            pl.BlockSpec((tm, kb), lambda i, j: (i, 0)),
            pl.BlockSpec((ka, tn), lambda i, j: (0, j)),
            pl.BlockSpec((kb, tn), lambda i, j: (0, j)),
            pl.BlockSpec((tm, tn), lambda i, j: (i, j)),
        ],
        out_specs=pl.BlockSpec((tm, tn), lambda i, j: (i, j)),
        out_shape=jax.ShapeDtypeStruct((t, n), F32),
        compiler_params=_cparams(("parallel", "arbitrary")),
        name="out_proj",
    )(a, b, wa, wb, res)


_RET_LOG_GAMMA = [math.log1p(-(2.0 ** (-5.0 - h))) for h in range(RET_HEADS)]


def _rope_tables(seq):
    inv = 1.0 / (ROPE_BASE ** (np.arange(0, RET_DIM, 2, dtype=np.float64) / RET_DIM))
    ang = np.arange(seq, dtype=np.float64)[:, None] * inv[None, :]
    cos = np.cos(ang)
    sin = np.sin(ang)
    cos2 = np.concatenate([cos, cos], axis=1).astype(np.float32)
    sin2 = np.concatenate([-sin, sin], axis=1).astype(np.float32)
    return jnp.asarray(cos2), jnp.asarray(sin2)


def _head_layer_norm(t, g):
    mu = jnp.mean(t, axis=-1, keepdims=True)
    tc = t - mu
    var = jnp.mean(tc * tc, axis=-1, keepdims=True)
    return tc * lax.rsqrt(var + EPS) * g


def _retention_kernel(q_ref, k_ref, v_ref, g_ref, cos_ref, sin_ref, ng_ref, o_ref, st_ref):
    @pl.when(pl.program_id(1) == 0)
    def _():
        st_ref[...] = jnp.zeros_like(st_ref)

    cos = cos_ref[...]
    sin = sin_ref[...]
    row = lax.broadcasted_iota(jnp.int32, (CHUNK, CHUNK), 0)
    col = lax.broadcasted_iota(jnp.int32, (CHUNK, CHUNK), 1)
    rel = (row - col).astype(F32)
    causal = row >= col
    pos = lax.broadcasted_iota(jnp.int32, (CHUNK, 1), 0).astype(F32)
    scale = RET_DIM ** -0.5
    for h in range(RET_HEADS):
        lg = _RET_LOG_GAMMA[h]
        sl = slice(h * RET_DIM, (h + 1) * RET_DIM)
        q = q_ref[:, sl]
        k = k_ref[:, sl]
        q = q * cos + pltpu.roll(q, RET_DIM // 2, 1) * sin
        k = (k * cos + pltpu.roll(k, RET_DIM // 2, 1) * sin) * scale
        qb = q.astype(BF16)
        kb = k.astype(BF16)
        vb = v_ref[:, sl].astype(BF16)
        decay = jnp.where(causal, jnp.exp(lg * jnp.maximum(rel, 0.0)), 0.0)
        scores = _dot_nt(qb, kb) * decay
        state = st_ref[h]
        out = _dot(scores.astype(BF16), vb)
        out = out + _dot(qb, state.astype(BF16)) * jnp.exp(lg * (pos + 1.0))
        kd = (k * jnp.exp(lg * (CHUNK - 1.0 - pos))).astype(BF16)
        st_ref[h] = math.exp(lg * CHUNK) * state + _dot_tn(kd, vb)
        y = _head_layer_norm(out, ng_ref[h:h + 1, :]) * _silu(g_ref[:, sl])
        o_ref[:, sl] = y.astype(o_ref.dtype)


def _retention(proj, norm_g, cos2, sin2):
    b, l, _ = proj.shape
    col_spec = lambda j: pl.BlockSpec((None, CHUNK, RET_WIDTH), lambda bi, c: (bi, c, j))
    tab_spec = pl.BlockSpec((CHUNK, RET_DIM), lambda bi, c: (c, 0))
    return pl.pallas_call(
        _retention_kernel,
        grid=(b, l // CHUNK),
        in_specs=[col_spec(0), col_spec(1), col_spec(2), col_spec(3), tab_spec, tab_spec,
                  pl.BlockSpec((RET_HEADS, RET_DIM), lambda bi, c: (0, 0))],
        out_specs=pl.BlockSpec((None, CHUNK, RET_WIDTH), lambda bi, c: (bi, c, 0)),
        out_shape=jax.ShapeDtypeStruct((b, l, RET_WIDTH), BF16),
        scratch_shapes=[pltpu.VMEM((RET_HEADS, RET_DIM, RET_DIM), F32)],
        compiler_params=_cparams(("parallel", "arbitrary")),
        name="retention",
    )(proj, proj, proj, proj, cos2, sin2, norm_g)


def _conv_reset(buf_ref):
    buf_ref[0:CONV_PAD, :] = jnp.zeros((CONV_PAD, buf_ref.shape[1]), F32)


def _conv_silu(x, buf_ref, w_ref, b_ref):
    n = x.shape[0]
    buf_ref[CONV_PAD:CONV_PAD + n, :] = x
    acc = b_ref[...] + w_ref[CONV_W - 1:CONV_W, :] * x
    for j in range(CONV_W - 1):
        off = CONV_PAD - (CONV_W - 1) + j
        acc = acc + w_ref[j:j + 1, :] * buf_ref[off:off + n, :]
    buf_ref[0:CONV_PAD, :] = x[n - CONV_PAD:, :]
    return _silu(acc)


def _ssd_kernel(z_ref, xs_ref, bc_ref, dt_ref, cwx_ref, cbx_ref, cwb_ref, cbb_ref, dtb_ref, alog_ref,
                dskip_ref, ng_ref, o_ref, st_ref, bufx_ref, bufb_ref):
    @pl.when(pl.program_id(1) == 0)
    def _():
        st_ref[...] = jnp.zeros_like(st_ref)
        _conv_reset(bufx_ref)
        _conv_reset(bufb_ref)

    xs = _conv_silu(xs_ref[...], bufx_ref, cwx_ref, cbx_ref)
    bc = _conv_silu(bc_ref[...], bufb_ref, cwb_ref, cbb_ref)
    dt = _softplus(dt_ref[...] + dtb_ref[...])
    da = dt * (-jnp.exp(alog_ref[...]))
    a_cum = _dot_exact(_tril_f32(CHUNK), da)
    a_cum_t = a_cum.T
    a_last = a_cum[CHUNK - 1:CHUNK, :]
    e_cum = jnp.exp(a_cum)
    to_end = jnp.exp(a_last - a_cum)
    e_last = jnp.exp(a_last)
    row = lax.broadcasted_iota(jnp.int32, (CHUNK, CHUNK), 0)
    col = lax.broadcasted_iota(jnp.int32, (CHUNK, CHUNK), 1)
    causal = row >= col
    gn = SSD_GROUPS * SSD_STATE
    for g in range(SSD_GROUPS):
        bm = bc[:, g * SSD_STATE:(g + 1) * SSD_STATE].astype(BF16)
        cm = bc[:, gn + g * SSD_STATE:gn + (g + 1) * SSD_STATE].astype(BF16)
        cb = _dot_nt(cm, bm)
        ys = []
        for r in range(SSD_HPG):
            hd = g * SSD_HPG + r
            sl = slice(hd * SSD_HEAD_DIM, (hd + 1) * SSD_HEAD_DIM)
            x_h = xs[:, sl]
            xdt = x_h * dt[:, hd:hd + 1]
            seg = jnp.exp(jnp.where(causal, a_cum[:, hd:hd + 1] - a_cum_t[hd:hd + 1, :], -jnp.inf))
            state = st_ref[hd]
            y = _dot((cb * seg).astype(BF16), xdt.astype(BF16))
            y = y + _dot(cm, state.astype(BF16)) * e_cum[:, hd:hd + 1]
            st_ref[hd] = e_last[:, hd:hd + 1] * state + _dot_tn(bm, (xdt * to_end[:, hd:hd + 1]).astype(BF16))
            ys.append(y + dskip_ref[:, sl] * x_h)
        gsl = slice(g * SSD_GW, (g + 1) * SSD_GW)
        y = jnp.concatenate(ys, axis=1) * _silu(z_ref[:, gsl])
        y = y * lax.rsqrt(jnp.mean(y * y, axis=-1, keepdims=True) + EPS)
        o_ref[:, gsl] = (y * ng_ref[:, gsl]).astype(o_ref.dtype)


def _ssd(proj, gates, conv_w, conv_b, dt_bias, a_log, d_skip, norm_g):
    b, l, _ = proj.shape
    bcw = 2 * SSD_GROUPS * SSD_STATE
    pad = lambda v: jnp.pad(v.astype(F32), (0, LANES - v.shape[0]))[None, :]
    full = lambda shape: pl.BlockSpec(shape, lambda bi, c: (0,) * len(shape))
    z_off = 4 * RET_WIDTH // SSD_WIDTH
    return pl.pallas_call(
        _ssd_kernel,
        grid=(b, l // CHUNK),
        in_specs=[
            pl.BlockSpec((None, CHUNK, SSD_WIDTH), lambda bi, c: (bi, c, z_off)),
            pl.BlockSpec((None, CHUNK, SSD_WIDTH), lambda bi, c: (bi, c, z_off + 1)),
            pl.BlockSpec((None, CHUNK, bcw), lambda bi, c: (bi, c, (z_off + 2) * SSD_WIDTH // bcw)),
            pl.BlockSpec((None, CHUNK, LANES), lambda bi, c: (bi, c, 0)),
            full((CONV_W, SSD_WIDTH)), full((1, SSD_WIDTH)), full((CONV_W, bcw)), full((1, bcw)),
            full((1, LANES)), full((1, LANES)), full((1, SSD_WIDTH)), full((1, SSD_WIDTH)),
        ],
        out_specs=pl.BlockSpec((None, CHUNK, SSD_WIDTH), lambda bi, c: (bi, c, 0)),
        out_shape=jax.ShapeDtypeStruct((b, l, SSD_WIDTH), BF16),
        scratch_shapes=[
            pltpu.VMEM((SSD_HEADS, SSD_STATE, SSD_HEAD_DIM), F32),
            pltpu.VMEM((CONV_PAD + CHUNK, SSD_WIDTH), F32),
            pltpu.VMEM((CONV_PAD + CHUNK, bcw), F32),
        ],
        compiler_params=_cparams(("parallel", "arbitrary")),
        name="ssd",
    )(proj, proj, proj, gates,
      conv_w[:, :SSD_WIDTH], conv_b[None, :SSD_WIDTH], conv_w[:, SSD_WIDTH:], conv_b[None, SSD_WIDTH:],
      pad(dt_bias), pad(a_log), jnp.repeat(d_skip.astype(F32), SSD_HEAD_DIM)[None, :], norm_g[None, :])


def _mlstm_kernel(q_ref, k_ref, v_ref, o_ref, gt_ref, cwq_ref, cbq_ref, cwk_ref, cbk_ref, gb_ref, ng_ref,
                  out_ref, c_ref, n_ref, m_ref, bufq_ref, bufk_ref):
    @pl.when(pl.program_id(1) == 0)
    def _():
        c_ref[...] = jnp.zeros_like(c_ref)
        n_ref[...] = jnp.zeros_like(n_ref)
        m_ref[...] = jnp.full(m_ref.shape, -jnp.inf, F32)
        _conv_reset(bufq_ref)
        _conv_reset(bufk_ref)

    nh = MLSTM_HEADS
    pre = gt_ref[...] + gb_ref[...]
    bcum_all = _dot_exact(_tril_f32(CHUNK), -_softplus(-pre))
    pre_t = pre.T
    bcum_all_t = bcum_all.T
    row = lax.broadcasted_iota(jnp.int32, (CHUNK, CHUNK), 0)
    col = lax.broadcasted_iota(jnp.int32, (CHUNK, CHUNK), 1)
    causal = row >= col

    for hd in range(nh):
        sl = slice(hd * MLSTM_DIM, (hd + 1) * MLSTM_DIM)
        q = _conv_silu(q_ref[:, sl], bufq_ref.at[:, sl], cwq_ref.at[:, sl], cbq_ref.at[:, sl]) * (MLSTM_DIM ** -0.5)
        k = _conv_silu(k_ref[:, sl], bufk_ref.at[:, sl], cwk_ref.at[:, sl], cbk_ref.at[:, sl])
        qb = q.astype(BF16)
        kb = k.astype(BF16)
        vb = v_ref[:, sl].astype(BF16)
        ig = pre[:, hd:hd + 1]
        ig_t = pre_t[hd:hd + 1, :]
        bcum = bcum_all[:, nh + hd:nh + hd + 1]
        bcum_t = bcum_all_t[nh + hd:nh + hd + 1, :]

        log_d = jnp.where(causal, bcum - bcum_t + ig_t, -jnp.inf)
        m_prev = m_ref[hd]
        m_inter = bcum + m_prev
        m_t = jnp.maximum(jnp.max(log_d, axis=1, keepdims=True), m_inter)
        scores = _dot_nt(qb, kb) * jnp.exp(log_d - m_t)
        inter_scale = jnp.exp(m_inter - m_t)
        c_st = c_ref[hd]
        n_st = n_ref[hd]
        num = _dot(scores.astype(BF16), vb) + inter_scale * _dot(qb, c_st.astype(BF16))
        den = jnp.sum(scores, axis=1, keepdims=True) + inter_scale * jnp.sum(q * n_st, axis=1, keepdims=True)
        hval = num / jnp.maximum(jnp.abs(den), jnp.exp(-m_t))

        b_last = bcum[CHUNK - 1:CHUNK, :]
        w_log = b_last - bcum + ig
        m_new = jnp.maximum(b_last + m_prev, jnp.max(w_log, axis=0, keepdims=True))
        kw = k * jnp.exp(w_log - m_new)
        prev_scale = jnp.exp(b_last + m_prev - m_new)
        c_ref[hd] = prev_scale * c_st + _dot_tn(kw.astype(BF16), vb)
        n_ref[hd] = prev_scale * n_st + jnp.sum(kw, axis=0, keepdims=True)
        m_ref[hd] = m_new

        y = _head_layer_norm(hval, ng_ref[:, sl]) * _sigmoid(o_ref[:, sl])
        out_ref[:, sl] = y.astype(out_ref.dtype)


def _mlstm(proj, gates, conv_w, conv_b, i_bias, f_bias, norm_g):
    b, l, _ = proj.shape
    nh = MLSTM_HEADS
    w = MLSTM_WIDTH
    col_spec = lambda j: pl.BlockSpec((None, CHUNK, w), lambda bi, c: (bi, c, j))
    cw_spec = lambda j: pl.BlockSpec((CONV_W, w), lambda bi, c: (0, j))
    cb_spec = lambda j: pl.BlockSpec((1, w), lambda bi, c: (0, j))
    gate_bias = jnp.pad(jnp.concatenate([i_bias, f_bias]).astype(F32), (0, LANES - 2 * nh))[None, :]
    return pl.pallas_call(
        _mlstm_kernel,
        grid=(b, l // CHUNK),
        in_specs=[
            col_spec(0), col_spec(1), col_spec(2), col_spec(3),
            pl.BlockSpec((None, CHUNK, LANES), lambda bi, c: (bi, c, 0)),
            cw_spec(0), cb_spec(0), cw_spec(1), cb_spec(1),
            pl.BlockSpec((1, LANES), lambda bi, c: (0, 0)),
            pl.BlockSpec((1, w), lambda bi, c: (0, 0)),
        ],
        out_specs=pl.BlockSpec((None, CHUNK, w), lambda bi, c: (bi, c, 0)),
        out_shape=jax.ShapeDtypeStruct((b, l, w), BF16),
        scratch_shapes=[
            pltpu.VMEM((nh, MLSTM_DIM, MLSTM_DIM), F32),
            pltpu.VMEM((nh, 1, MLSTM_DIM), F32),
            pltpu.VMEM((nh, 1, 1), F32),
            pltpu.VMEM((CONV_PAD + CHUNK, w), F32),
            pltpu.VMEM((CONV_PAD + CHUNK, w), F32),
        ],
        compiler_params=_cparams(("parallel", "arbitrary")),
        name="mlstm",
    )(proj, proj, proj, proj, gates, conv_w, conv_b[None, :], conv_w, conv_b[None, :], gate_bias,
      norm_g.reshape(1, w))


S5_SEG = CHUNK // CONV_PAD
S5_SLAB = 128
S5_SLAB_LANES = S5_SLAB // S5_GROUP * S5_STATE
_S5_SEG_LOG = int(math.log2(CONV_PAD))


def _cmul(a_re, a_im, b_re, b_im):
    return a_re * b_re - a_im * b_im, a_re * b_im + a_im * b_re


def _s5_kernel(u_ref, bre_ref, bim_ref, pt_re_ref, pt_im_ref, sg_re_ref, sg_im_ref, cre_ref, cim_ref,
               dskip_ref, wglu_ref, bglu_ref, o_ref, sre_ref, sim_ref, st_re_ref, st_im_ref, ucol_ref):
    @pl.when(pl.program_id(1) == 0)
    def _():
        st_re_ref[...] = jnp.zeros_like(st_re_ref)
        st_im_ref[...] = jnp.zeros_like(st_im_ref)

    for q in range(S5_WIDTH // LANES):
        ucol_ref[q] = u_ref[:, q * LANES:(q + 1) * LANES]
    u = jnp.concatenate(
        [jnp.concatenate([ucol_ref[q, pl.ds(v, CONV_PAD, stride=S5_SEG), :] for v in range(S5_SEG)], axis=0)
         for q in range(S5_WIDTH // LANES)], axis=1)
    ub = u.astype(BF16)
    sub = lax.broadcasted_iota(jnp.int32, (CONV_PAD, LANES), 0)
    grp = lambda x, v: x[v * CONV_PAD:(v + 1) * CONV_PAD, :]
    for jj in range(S5_WIDTH // S5_SLAB):
        ch = slice(jj * S5_SLAB, (jj + 1) * S5_SLAB)
        xr = _dot(ub[:, ch], bre_ref[ch, jj * S5_SLAB_LANES:(jj + 1) * S5_SLAB_LANES])
        xi = _dot(ub[:, ch], bim_ref[ch, jj * S5_SLAB_LANES:(jj + 1) * S5_SLAB_LANES])
        for q in range(S5_SLAB_LANES // LANES):
            sl = slice(jj * S5_SLAB_LANES + q * LANES, jj * S5_SLAB_LANES + (q + 1) * LANES)
            lre = xr[:, q * LANES:(q + 1) * LANES]
            lim = xi[:, q * LANES:(q + 1) * LANES]
            a_re = pt_re_ref[0:CONV_PAD, sl]
            a_im = pt_im_ref[0:CONV_PAD, sl]
            s_re = [grp(lre, 0)]
            s_im = [grp(lim, 0)]
            for v in range(1, S5_SEG):
                d_re, d_im = _cmul(a_re, a_im, s_re[-1], s_im[-1])
                s_re.append(grp(lre, v) + d_re)
                s_im.append(grp(lim, v) + d_im)
            c_re = st_re_ref[:, sl]
            c_im = st_im_ref[:, sl]
            i_re, i_im = _cmul(sg_re_ref[0:1, sl], sg_im_ref[0:1, sl], c_re, c_im)
            e_re = s_re[-1] + jnp.where(sub == 0, i_re, 0.0)
            e_im = s_im[-1] + jnp.where(sub == 0, i_im, 0.0)
            for k in range(_S5_SEG_LOG):
                d = 1 << k
                h_re = jnp.where(sub >= d, pltpu.roll(e_re, d, 0), 0.0)
                h_im = jnp.where(sub >= d, pltpu.roll(e_im, d, 0), 0.0)
                d_re, d_im = _cmul(sg_re_ref[k:k + 1, sl], sg_im_ref[k:k + 1, sl], h_re, h_im)
                e_re = e_re + d_re
                e_im = e_im + d_im
            st_re_ref[:, sl] = e_re[CONV_PAD - 1:CONV_PAD, :]
            st_im_ref[:, sl] = e_im[CONV_PAD - 1:CONV_PAD, :]
            p_re = jnp.where(sub >= 1, pltpu.roll(e_re, 1, 0), c_re)
            p_im = jnp.where(sub >= 1, pltpu.roll(e_im, 1, 0), c_im)
            for v in range(S5_SEG):
                rows = slice(v * CONV_PAD, (v + 1) * CONV_PAD)
                d_re, d_im = _cmul(pt_re_ref[rows, sl], pt_im_ref[rows, sl], p_re, p_im)
                s_re[v] = s_re[v] + d_re
                s_im[v] = s_im[v] + d_im
            sre_ref[:, sl] = jnp.concatenate(s_re, axis=0).astype(BF16)
            sim_ref[:, sl] = jnp.concatenate(s_im, axis=0).astype(BF16)

    ys = []
    for jj in range(S5_WIDTH // S5_SLAB):
        ch = slice(jj * S5_SLAB, (jj + 1) * S5_SLAB)
        ln = slice(jj * S5_SLAB_LANES, (jj + 1) * S5_SLAB_LANES)
        ys.append(_dot(sre_ref[:, ln], cre_ref[ln, ch]) - _dot(sim_ref[:, ln], cim_ref[ln, ch]))
    y = jnp.concatenate(ys, axis=1) + dskip_ref[...] * u
    y = 0.5 * y * (1.0 + jnp.tanh(math.sqrt(2.0 / math.pi) * (y + 0.044715 * (y * y * y))))
    vg = _dot(y.astype(BF16), wglu_ref[...]) + bglu_ref[...]
    res = (vg[:, :S5_WIDTH] * _sigmoid(vg[:, S5_WIDTH:])).astype(BF16)
    t_idx = lax.broadcasted_iota(jnp.int32, (CHUNK, CHUNK), 0)
    p_idx = lax.broadcasted_iota(jnp.int32, (CHUNK, CHUNK), 1)
    unperm = jnp.where(p_idx == CONV_PAD * (t_idx % S5_SEG) + t_idx // S5_SEG, 1.0, 0.0).astype(BF16)
    o_ref[...] = _dot(unperm, res).astype(o_ref.dtype)


def _s5_params(a_re, a_im, log_dt, b_re, b_im, c_re, c_im):
    a_re = a_re.astype(F32)
    a_im = a_im.astype(F32)
    dt = jnp.exp(log_dt.astype(F32))[:, None]
    mag = jnp.exp(a_re * dt)
    abar_re = mag * jnp.cos(a_im * dt)
    abar_im = mag * jnp.sin(a_im * dt)
    den = a_re * a_re + a_im * a_im
    nr = abar_re - 1.0
    coef_re = (nr * a_re + abar_im * a_im) / den
    coef_im = (abar_im * a_re - nr * a_im) / den
    b_re = b_re.astype(F32)
    b_im = b_im.astype(F32)
    bbar_re = coef_re[..., None] * b_re - coef_im[..., None] * b_im
    bbar_im = coef_re[..., None] * b_im + coef_im[..., None] * b_re
    eye = jnp.eye(S5_GROUPS, dtype=F32)
    in_map = lambda m: jnp.einsum('gnc,gh->gchn', m, eye).reshape(S5_WIDTH, S5_LANES).astype(BF16)
    out_map = lambda m: jnp.einsum('gcn,gh->gnhc', m.astype(F32), eye).reshape(S5_LANES, S5_WIDTH).astype(BF16)

    def powers(exps):
        e = exps[:, None, None]
        m = jnp.exp(e * (a_re * dt)[None])
        ang = e * (a_im * dt)[None]
        return (m * jnp.cos(ang)).reshape(-1, S5_LANES), (m * jnp.sin(ang)).reshape(-1, S5_LANES)

    pt_re, pt_im = powers(jnp.repeat(jnp.arange(1, S5_SEG + 1, dtype=F32), CONV_PAD))
    sg_re, sg_im = powers(jnp.asarray([float(S5_SEG << k) for k in range(_S5_SEG_LOG)], F32))
    pad8 = lambda m: jnp.pad(m, ((0, CONV_PAD - m.shape[0]), (0, 0)))
    return (in_map(bbar_re), in_map(bbar_im), pt_re, pt_im, pad8(sg_re), pad8(sg_im),
            out_map(c_re), out_map(c_im))


def _s5(proj, a_re, a_im, log_dt, b_re, b_im, c_re, c_im, d_skip, w_glu, b_glu):
    b, l, _ = proj.shape
    tabs = _s5_params(a_re, a_im, log_dt, b_re, b_im, c_re, c_im)
    full = lambda shape: pl.BlockSpec(shape, lambda bi, c: (0,) * len(shape))
    u_off = 4 * MLSTM_WIDTH // S5_WIDTH
    return pl.pallas_call(
        _s5_kernel,
        grid=(b, l // CHUNK),
        in_specs=[
            pl.BlockSpec((None, CHUNK, S5_WIDTH), lambda bi, c: (bi, c, u_off)),
            full((S5_WIDTH, S5_LANES)), full((S5_WIDTH, S5_LANES)),
            full((CHUNK, S5_LANES)), full((CHUNK, S5_LANES)),
            full((CONV_PAD, S5_LANES)), full((CONV_PAD, S5_LANES)),
            full((S5_LANES, S5_WIDTH)), full((S5_LANES, S5_WIDTH)),
            full((1, S5_WIDTH)), full((S5_WIDTH, 2 * S5_WIDTH)), full((1, 2 * S5_WIDTH)),
        ],
        out_specs=pl.BlockSpec((None, CHUNK, S5_WIDTH), lambda bi, c: (bi, c, 0)),
        out_shape=jax.ShapeDtypeStruct((b, l, S5_WIDTH), BF16),
        scratch_shapes=[
            pltpu.VMEM((CHUNK, S5_LANES), BF16), pltpu.VMEM((CHUNK, S5_LANES), BF16),
            pltpu.VMEM((1, S5_LANES), F32), pltpu.VMEM((1, S5_LANES), F32),
            pltpu.VMEM((S5_WIDTH // LANES, CHUNK, LANES), F32),
        ],
        compiler_params=_cparams(("parallel", "arbitrary")),
        name="s5",
    )(proj, *tabs, d_skip.astype(F32).reshape(1, S5_WIDTH), w_glu.astype(BF16), b_glu.astype(F32)[None, :])


def _router_kernel(x_ref, g_ref, w_ref, b_ref, ids_ref, wts_ref, cnt_ref, carry_ref):
    @pl.when(pl.program_id(0) == 0)
    def _():
        carry_ref[...] = jnp.zeros_like(carry_ref)

    x = x_ref[...]
    xn = x * lax.rsqrt(jnp.mean(x * x, axis=-1, keepdims=True) + EPS) * g_ref[...]
    logits = _dot_exact(xn, w_ref[...]) + b_ref[...]
    lane = lax.broadcasted_iota(jnp.int32, logits.shape, 1)
    big = jnp.int32(LANES)
    neg = -jnp.inf

    def top(vals):
        m = jnp.max(vals, axis=1, keepdims=True)
        idx = jnp.min(jnp.where(vals == m, lane, big), axis=1, keepdims=True)
        return m, idx

    gl = jnp.where(lane < MOE_GROUPS, logits, neg)
    g_max, g_idx = top(gl)
    g_w = 1.0 / jnp.sum(jnp.exp(gl - g_max), axis=1, keepdims=True)
    lo = MOE_GROUPS + EXPERTS_PER_GROUP * g_idx
    el = jnp.where((lane >= lo) & (lane < lo + EXPERTS_PER_GROUP), logits, neg)
    m1, j1 = top(el)
    m2, j2 = top(jnp.where(lane == j1, neg, el))
    e2 = jnp.exp(m2 - m1)
    w1 = g_w / (1.0 + e2)
    w2 = g_w * e2 / (1.0 + e2)
    e1 = j1 - MOE_GROUPS
    e2 = j2 - MOE_GROUPS

    tm = x.shape[0]
    hit = jnp.where(lane == e1, 1.0, jnp.where(lane == e2, 1.0, 0.0))
    row = lax.broadcasted_iota(jnp.int32, (tm, tm), 0)
    col = lax.broadcasted_iota(jnp.int32, (tm, tm), 1)
    before = jnp.where(row > col, 1.0, 0.0).astype(BF16)
    prefix = _dot(before, hit.astype(BF16)) + carry_ref[...]
    r1 = jnp.sum(jnp.where(lane == e1, prefix, 0.0), axis=1, keepdims=True).astype(jnp.int32)
    r2 = jnp.sum(jnp.where(lane == e2, prefix, 0.0), axis=1, keepdims=True).astype(jnp.int32)
    carry_ref[...] = carry_ref[...] + jnp.sum(hit, axis=0, keepdims=True)
    cnt_ref[...] = carry_ref[...]

    ids_ref[...] = jnp.where(lane == 0, e1, jnp.where(lane == 1, e2, jnp.where(lane == 2, r1, jnp.where(lane == 3, r2, 0))))
    wts_ref[...] = jnp.where(lane == 0, w1, jnp.where(lane == 1, w2, 0.0))


def _router(x, g, w, bias, tm=512):
    t, d = x.shape
    return pl.pallas_call(
        _router_kernel,
        grid=(t // tm,),
        in_specs=[
            pl.BlockSpec((tm, d), lambda i: (i, 0)),
            pl.BlockSpec((1, d), lambda i: (0, 0)),
            pl.BlockSpec((d, LANES), lambda i: (0, 0)),
            pl.BlockSpec((1, LANES), lambda i: (0, 0)),
        ],
        out_specs=[pl.BlockSpec((tm, LANES), lambda i: (i, 0)), pl.BlockSpec((tm, LANES), lambda i: (i, 0)),
                   pl.BlockSpec((1, LANES), lambda i: (0, 0))],
        out_shape=[jax.ShapeDtypeStruct((t, LANES), jnp.int32), jax.ShapeDtypeStruct((t, LANES), F32),
                   jax.ShapeDtypeStruct((1, LANES), F32)],
        scratch_shapes=[pltpu.VMEM((1, LANES), F32)],
        compiler_params=_cparams(("arbitrary",)),
        name="moe_router",
    )(x, g, w, bias)


def _row_gather_copy(src_hbm, idx_ref, base, buf_ref, slot, sem_ref, r):
    return pltpu.make_async_copy(src_hbm.at[pl.ds(idx_ref[base + r], 1)], buf_ref.at[slot, pl.ds(r, 1)],
                                 sem_ref.at[slot])


def _start_row_gather(src_hbm, idx_ref, base, buf_ref, slot, sem_ref, rows):
    def body(r, carry):
        _row_gather_copy(src_hbm, idx_ref, base, buf_ref, slot, sem_ref, r).start()
        return carry

    lax.fori_loop(0, rows, body, 0)


def _start_row_gather_inline(src_hbm, idx_ref, base, buf_ref, slot, sem_ref, rows, alternate):
    for r in range(rows):
        _row_gather_copy(src_hbm, idx_ref, base, buf_ref, slot, sem_ref, r).start(priority=r % 2 if alternate else 0)


def _wait_row_gather(src_hbm, buf_ref, slot, sem_ref, rows):
    pltpu.make_async_copy(src_hbm.at[pl.ds(0, rows)], buf_ref.at[slot], sem_ref.at[slot]).wait()


def _expert_kernel(be_ref, first_ref, par_ref, nxt_ref, src_ref, nused_ref, x_hbm, g_ref, wg_hbm, wu_hbm, wd_hbm,
                   o_ref, xbuf_ref, xsem_ref, wgf_ref, wuf_ref, wdf_ref, wsem_ref, wgb_ref, wub_ref, wdb_ref, *,
                   layer):
    i = pl.program_id(0)
    n = pl.num_programs(0)
    n_used = nused_ref[0]
    slot = lax.rem(i, 2)
    next_base = lax.rem(i + 1, n) * MOE_ROWS

    def weight_copies(e, p):
        return (pltpu.make_async_copy(wg_hbm.at[layer, e], wgf_ref.at[p], wsem_ref.at[0, p]),
                pltpu.make_async_copy(wu_hbm.at[layer, e], wuf_ref.at[p], wsem_ref.at[1, p]),
                pltpu.make_async_copy(wd_hbm.at[layer, e], wdf_ref.at[p], wsem_ref.at[2, p]))

    @pl.when(i == 0)
    def _():
        _start_row_gather(x_hbm, src_ref, 0, xbuf_ref, 0, xsem_ref, MOE_ROWS)
        for c in weight_copies(be_ref[0], 0):
            c.start(priority=1)

    @pl.when((i < n_used) & (first_ref[i] == 1))
    def _():
        p = par_ref[i]
        for c in weight_copies(be_ref[i], p):
            c.wait()
        nxt = nxt_ref[i]

        @pl.when(nxt >= 0)
        def _():
            for c in weight_copies(nxt, 1 - p):
                c.start(priority=1)

        wgb_ref[...] = wgf_ref[p].astype(BF16)
        wub_ref[...] = wuf_ref[p].astype(BF16)
        wdb_ref[...] = wdf_ref[p].astype(BF16)

    _wait_row_gather(x_hbm, xbuf_ref, slot, xsem_ref, MOE_ROWS)

    @pl.when(i < n_used)
    def _():
        _start_row_gather_inline(x_hbm, src_ref, next_base, xbuf_ref, 1 - slot, xsem_ref, MOE_ROWS, False)
        x = xbuf_ref[slot]
        xn = (x * lax.rsqrt(jnp.mean(x * x, axis=-1, keepdims=True) + EPS) * g_ref[...]).astype(BF16)
        act = (_silu(_dot(xn, wgb_ref[...])) * _dot(xn, wub_ref[...])).astype(BF16)
        o_ref[...] = _dot(act, wdb_ref[...])

    @pl.when(i >= n_used)
    def _():
        _start_row_gather(x_hbm, src_ref, next_base, xbuf_ref, 1 - slot, xsem_ref, MOE_ROWS)
        o_ref[...] = jnp.zeros_like(o_ref)

    @pl.when(i == n - 1)
    def _():
        _wait_row_gather(x_hbm, xbuf_ref, 1 - slot, xsem_ref, MOE_ROWS)


def _expert_ffn(x, g, tables, src_tok, n_used, w_gate, w_up, w_down, layer):
    t, d = x.shape
    n_rows = src_tok.shape[0]
    n_blocks = n_rows // MOE_ROWS
    ff = w_gate.shape[-1]
    any_spec = pl.BlockSpec(memory_space=pl.ANY)
    grid_spec = pltpu.PrefetchScalarGridSpec(
        num_scalar_prefetch=6,
        grid=(n_blocks,),
        in_specs=[any_spec, pl.BlockSpec((1, d), lambda i, *_: (0, 0)), any_spec, any_spec, any_spec],
        out_specs=pl.BlockSpec((MOE_ROWS, d), lambda i, *_: (i, 0)),
        scratch_shapes=[
            pltpu.VMEM((2, MOE_ROWS, d), F32),
            pltpu.SemaphoreType.DMA((2,)),
            pltpu.VMEM((2, d, ff), F32), pltpu.VMEM((2, d, ff), F32), pltpu.VMEM((2, ff, d), F32),
            pltpu.SemaphoreType.DMA((3, 2)),
            pltpu.VMEM((d, ff), BF16), pltpu.VMEM((d, ff), BF16), pltpu.VMEM((ff, d), BF16),
        ],
    )
    return pl.pallas_call(
        functools.partial(_expert_kernel, layer=layer),
        grid_spec=grid_spec,
        out_shape=jax.ShapeDtypeStruct((n_rows, d), F32),
        compiler_params=_cparams(("arbitrary",)),
        name="moe_experts",
    )(*tables, src_tok, n_used, x, g, w_gate, w_up, w_down)


def _combine_kernel(d0_ref, d1_ref, y_hbm, h_ref, w_ref, fg_ref, o_ref, buf0_ref, buf1_ref, sem0_ref, sem1_ref, *,
                    final_norm):
    i = pl.program_id(0)
    n = pl.num_programs(0)
    slot = lax.rem(i, 2)
    next_base = lax.rem(i + 1, n) * COMBINE_ROWS

    @pl.when(i == 0)
    def _():
        _start_row_gather(y_hbm, d0_ref, 0, buf0_ref, 0, sem0_ref, COMBINE_ROWS)
        _start_row_gather(y_hbm, d1_ref, 0, buf1_ref, 0, sem1_ref, COMBINE_ROWS)

    _wait_row_gather(y_hbm, buf0_ref, slot, sem0_ref, COMBINE_ROWS)
    _wait_row_gather(y_hbm, buf1_ref, slot, sem1_ref, COMBINE_ROWS)
    _start_row_gather_inline(y_hbm, d0_ref, next_base, buf0_ref, 1 - slot, sem0_ref, COMBINE_ROWS, True)
    _start_row_gather_inline(y_hbm, d1_ref, next_base, buf1_ref, 1 - slot, sem1_ref, COMBINE_ROWS, True)
    w = w_ref[...]
    y = h_ref[...] + (w[:, 0:1] * buf0_ref[slot] + w[:, 1:2] * buf1_ref[slot])
    if final_norm:
        y = y * lax.rsqrt(jnp.mean(y * y, axis=-1, keepdims=True) + EPS) * fg_ref[...]
    o_ref[...] = y

    @pl.when(i == n - 1)
    def _():
        _wait_row_gather(y_hbm, buf0_ref, 1 - slot, sem0_ref, COMBINE_ROWS)
        _wait_row_gather(y_hbm, buf1_ref, 1 - slot, sem1_ref, COMBINE_ROWS)


def _combine(y_rows, dest0, dest1, h, wts, final_g, final_norm):
    t, d = h.shape
    grid_spec = pltpu.PrefetchScalarGridSpec(
        num_scalar_prefetch=2,
        grid=(t // COMBINE_ROWS,),
        in_specs=[
            pl.BlockSpec(memory_space=pl.ANY),
            pl.BlockSpec((COMBINE_ROWS, d), lambda i, d0, d1: (i, 0)),
            pl.BlockSpec((COMBINE_ROWS, LANES), lambda i, d0, d1: (i, 0)),
            pl.BlockSpec((1, d), lambda i, d0, d1: (0, 0)),
        ],
        out_specs=pl.BlockSpec((COMBINE_ROWS, d), lambda i, d0, d1: (i, 0)),
        scratch_shapes=[
            pltpu.VMEM((2, COMBINE_ROWS, d), F32), pltpu.VMEM((2, COMBINE_ROWS, d), F32),
            pltpu.SemaphoreType.DMA((2,)), pltpu.SemaphoreType.DMA((2,)),
        ],
    )
    return pl.pallas_call(
        functools.partial(_combine_kernel, final_norm=final_norm),
        grid_spec=grid_spec,
        out_shape=jax.ShapeDtypeStruct((t, d), F32),
        compiler_params=_cparams(("arbitrary",)),
        name="moe_combine",
    )(dest0, dest1, y_rows, h, wts, final_g)


def _moe(h, layer, norm_g, router_g, router_g_b, router_e, router_e_b, w_gate, w_up, w_down, final_g, final_norm):
    t, d = h.shape
    g = norm_g.astype(F32)[None, :]
    pad_cols = LANES - MOE_GROUPS - N_EXPERTS
    w_r = jnp.pad(jnp.concatenate([router_g, router_e], axis=1).astype(F32), ((0, 0), (0, pad_cols)))
    b_r = jnp.pad(jnp.concatenate([router_g_b, router_e_b]).astype(F32), (0, pad_cols))[None, :]
    ids, wts, cnt = _router(h, g, w_r, b_r)

    n_slots = t * MOE_TOP_K
    n_blocks = (n_slots + N_EXPERTS * (MOE_ROWS - 1) + MOE_ROWS - 1) // MOE_ROWS
    n_rows = n_blocks * MOE_ROWS
    counts = cnt[0, :N_EXPERTS].astype(jnp.int32)
    blocks_of = (counts + MOE_ROWS - 1) // MOE_ROWS
    blk_end = jnp.cumsum(blocks_of)
    blk_start = blk_end - blocks_of
    n_used = blk_end[-1:]
    blk = jnp.arange(n_blocks, dtype=jnp.int32)
    block_expert = jnp.minimum(jnp.sum((blk_end[None, :] <= blk[:, None]).astype(jnp.int32), axis=1), N_EXPERTS - 1)
    first = (blk == blk_start[block_expert]).astype(jnp.int32)
    parity = (jnp.cumsum(first) - 1) % 2
    nxt_blk = blk_end[block_expert]
    nxt_expert = jnp.where(nxt_blk < n_used[0], block_expert[jnp.minimum(nxt_blk, n_blocks - 1)], -1)
    tables = (block_expert, first, parity.astype(jnp.int32), nxt_expert.astype(jnp.int32))

    dest = (blk_start * MOE_ROWS)[ids[:, 0:MOE_TOP_K]] + ids[:, MOE_TOP_K:2 * MOE_TOP_K]
    tok = jnp.arange(n_slots, dtype=jnp.int32) // MOE_TOP_K
    src_tok = jnp.zeros((n_rows,), jnp.int32).at[dest.reshape(n_slots)].set(tok, unique_indices=True)

    y_rows = _expert_ffn(h, g, tables, src_tok, n_used, w_gate, w_up, w_down, layer)
    return _combine(y_rows, dest[:, 0], dest[:, 1], h, wts, final_g.astype(F32)[None, :], final_norm)


def _split_in_proj(w_in, gate_lo, gate_hi):
    main = jnp.concatenate([w_in[:, :gate_lo], w_in[:, gate_hi:]], axis=1).astype(BF16)
    small = jnp.pad(w_in[:, gate_lo:gate_hi], ((0, 0), (0, LANES - (gate_hi - gate_lo)))).astype(BF16)
    return main, small


def _even_layer(h, b, l, norm_g, w_in, ret_norm, conv_w, conv_b, dt_bias, a_log, d_skip, ssd_norm, w_out):
    w_main, w_small = _split_in_proj(w_in, MAIN_WIDTH, MAIN_WIDTH + SSD_HEADS)
    proj, gates = _norm_matmul(h, norm_g.astype(F32)[None, :], w_main, w_small)
    proj = proj.reshape(b, l, MAIN_WIDTH)
    gates = gates.reshape(b, l, LANES)
    cos2, sin2 = _rope_tables(l)
    a_out = _retention(proj, ret_norm.astype(F32), cos2, sin2)
    b_out = _ssd(proj, gates, conv_w.astype(F32), conv_b.astype(F32), dt_bias, a_log, d_skip, ssd_norm.astype(F32))
    w_out = w_out.astype(BF16)
    return _out_proj(a_out.reshape(b * l, RET_WIDTH), b_out.reshape(b * l, SSD_WIDTH),
                     w_out[:RET_WIDTH], w_out[RET_WIDTH:], h)


def _odd_layer(h, b, l, norm_g, w_in, conv_w, conv_b, i_bias, f_bias, mlstm_norm, a_re, a_im, log_dt, b_re, b_im,
               c_re, c_im, d_skip, w_glu, b_glu, w_out):
    gate_lo = 4 * MLSTM_WIDTH
    w_main, w_small = _split_in_proj(w_in, gate_lo, gate_lo + 2 * MLSTM_HEADS)
    proj, gates = _norm_matmul(h, norm_g.astype(F32)[None, :], w_main, w_small)
    proj = proj.reshape(b, l, MAIN_WIDTH)
    gates = gates.reshape(b, l, LANES)
    c_out = _mlstm(proj, gates, conv_w.astype(F32), conv_b.astype(F32), i_bias, f_bias, mlstm_norm.astype(F32))
    d_out = _s5(proj, a_re, a_im, log_dt, b_re, b_im, c_re, c_im, d_skip, w_glu, b_glu)
    w_out = w_out.astype(BF16)
    return _out_proj(c_out.reshape(b * l, MLSTM_WIDTH), d_out.reshape(b * l, S5_WIDTH),
                     w_out[:MLSTM_WIDTH], w_out[MLSTM_WIDTH:], h)


def kernel(x, even_norm, even_w_in, ret_norm, ssd_conv_w, ssd_conv_b, ssd_dt_bias, ssd_a_log, ssd_d, ssd_norm, even_w_out, odd_norm, odd_w_in, mlstm_conv_w, mlstm_conv_b, mlstm_i_bias, mlstm_f_bias, mlstm_norm, s5_a_re, s5_a_im, s5_log_dt, s5_b_re, s5_b_im, s5_c_re, s5_c_im, s5_d, s5_w_glu, s5_b_glu, odd_w_out, moe_norm, moe_router_g, moe_router_g_b, moe_router_e, moe_router_e_b, moe_w_gate, moe_w_up, moe_w_down, final_norm):
    b, l, d = x.shape
    depth = moe_norm.shape[0]
    h = x.reshape(b * l, d)
    for layer in range(depth):
        i = layer // 2
        if layer % 2 == 0:
            h = _even_layer(h, b, l, even_norm[i], even_w_in[i], ret_norm[i], ssd_conv_w[i], ssd_conv_b[i],
                            ssd_dt_bias[i], ssd_a_log[i], ssd_d[i], ssd_norm[i], even_w_out[i])
        else:
            h = _odd_layer(h, b, l, odd_norm[i], odd_w_in[i], mlstm_conv_w[i], mlstm_conv_b[i], mlstm_i_bias[i],
                           mlstm_f_bias[i], mlstm_norm[i], s5_a_re[i], s5_a_im[i], s5_log_dt[i], s5_b_re[i],
                           s5_b_im[i], s5_c_re[i], s5_c_im[i], s5_d[i], s5_w_glu[i], s5_b_glu[i], odd_w_out[i])
        h = _moe(h, layer, moe_norm[layer], moe_router_g[layer], moe_router_g_b[layer], moe_router_e[layer],
                 moe_router_e_b[layer], moe_w_gate, moe_w_up, moe_w_down, final_norm,
                 final_norm=(layer == depth - 1))
    return h.reshape(b, l, d)
```

```python
import functools
import math

import numpy as np
import jax
import jax.numpy as jnp
from jax import lax
from jax.experimental import pallas as pl
from jax.experimental.pallas import tpu as pltpu

F32 = jnp.float32
BF16 = jnp.bfloat16
HIGHEST = lax.Precision.HIGHEST

D_MODEL = 2048
CHUNK = 128
CONV_W = 4
EPS = 1e-6
LANES = 128
CONV_PAD = 8

RET_HEADS = 8
RET_DIM = 128
RET_WIDTH = RET_HEADS * RET_DIM
ROPE_BASE = 10000.0
SSD_HEAD_DIM = 64
SSD_HEADS = 16
SSD_WIDTH = SSD_HEADS * SSD_HEAD_DIM
SSD_GROUPS = 2
SSD_HPG = SSD_HEADS // SSD_GROUPS
SSD_STATE = 128
SSD_GW = SSD_WIDTH // SSD_GROUPS
MAIN_WIDTH = 6656
MLSTM_HEADS = 4
MLSTM_DIM = 384
MLSTM_WIDTH = MLSTM_HEADS * MLSTM_DIM
S5_GROUP = 16
S5_GROUPS = 32
S5_WIDTH = S5_GROUP * S5_GROUPS
S5_STATE = 64
S5_LANES = S5_GROUPS * S5_STATE
MOE_GROUPS = 4
EXPERTS_PER_GROUP = 8
N_EXPERTS = MOE_GROUPS * EXPERTS_PER_GROUP
MOE_TOP_K = 2
EXPERT_FF = 512
MOE_ROWS = 256
COMBINE_ROWS = 128

VMEM_LIMIT = 56 * 1024 * 1024


def _cparams(sem):
    return pltpu.CompilerParams(dimension_semantics=sem, vmem_limit_bytes=VMEM_LIMIT)


def _dot(a, b):
    return jnp.dot(a, b, preferred_element_type=F32)


def _dot_nt(a, b):
    return lax.dot_general(a, b, (((1,), (1,)), ((), ())), preferred_element_type=F32)


def _dot_tn(a, b):
    return lax.dot_general(a, b, (((0,), (0,)), ((), ())), preferred_element_type=F32)


def _dot_exact(a, b):
    return jnp.dot(a, b, preferred_element_type=F32, precision=HIGHEST)


def _sigmoid(x):
    return 1.0 / (1.0 + jnp.exp(-x))


def _silu(x):
    return x * _sigmoid(x)


def _softplus(x):
    return jnp.maximum(x, 0.0) + jnp.log1p(jnp.exp(-jnp.abs(x)))


def _tril_f32(n):
    r = lax.broadcasted_iota(jnp.int32, (n, n), 0)
    c = lax.broadcasted_iota(jnp.int32, (n, n), 1)
    return (r >= c).astype(F32)


def _norm_matmul_kernel(x_ref, g_ref, w_ref, w2_ref, o_ref, o2_ref, xn_ref):
    @pl.when(pl.program_id(1) == 0)
    def _():
        x = x_ref[...]
        ms = jnp.mean(x * x, axis=-1, keepdims=True)
        xn = (x * lax.rsqrt(ms + EPS) * g_ref[...]).astype(BF16)
        xn_ref[...] = xn
        o2_ref[...] = _dot(xn, w2_ref[...])

    o_ref[...] = _dot(xn_ref[...], w_ref[...]).astype(o_ref.dtype)


def _norm_matmul(x, g, w, w2, tm=1024, tn=1664):
    t, d = x.shape
    n = w.shape[1]
    tm = min(tm, t)
    return pl.pallas_call(
        _norm_matmul_kernel,
        grid=(t // tm, n // tn),
        in_specs=[
            pl.BlockSpec((tm, d), lambda i, j: (i, 0)),
            pl.BlockSpec((1, d), lambda i, j: (0, 0)),
            pl.BlockSpec((d, tn), lambda i, j: (0, j)),
            pl.BlockSpec((d, LANES), lambda i, j: (0, 0)),
        ],
        out_specs=[
            pl.BlockSpec((tm, tn), lambda i, j: (i, j)),
            pl.BlockSpec((tm, LANES), lambda i, j: (i, 0)),
        ],
        out_shape=[jax.ShapeDtypeStruct((t, n), BF16), jax.ShapeDtypeStruct((t, LANES), F32)],
        scratch_shapes=[pltpu.VMEM((tm, d), BF16)],
        compiler_params=_cparams(("parallel", "arbitrary")),
        name="norm_in_proj",
    )(x, g, w, w2)


def _out_proj_kernel(a_ref, w_ref, r_ref, o_ref):
    o_ref[...] = r_ref[...] + _dot(a_ref[...], w_ref[...])


def _out_proj(a, w, res, tm=1024, tn=1024):
    t, k = a.shape
    n = w.shape[1]
    tm = min(tm, t)
    return pl.pallas_call(
        _out_proj_kernel,
        grid=(t // tm, n // tn),
        in_specs=[
            pl.BlockSpec((tm, k), lambda i, j: (i, 0)),
            pl.BlockSpec((k, tn), lambda i, j: (0, j)),
            pl.BlockSpec((tm, tn), lambda i, j: (i, j)),
        ],
        out_specs=pl.BlockSpec((tm, tn), lambda i, j: (i, j)),
        out_shape=jax.ShapeDtypeStruct((t, n), F32),
        compiler_params=_cparams(("parallel", "arbitrary")),
        name="out_proj",
    )(a, w, res)


_RET_LOG_GAMMA = [math.log1p(-(2.0 ** (-5.0 - h))) for h in range(RET_HEADS)]


def _rope_tables(seq):
    inv = 1.0 / (ROPE_BASE ** (np.arange(0, RET_DIM, 2, dtype=np.float64) / RET_DIM))
    ang = np.arange(seq, dtype=np.float64)[:, None] * inv[None, :]
    cos = np.cos(ang)
    sin = np.sin(ang)
    cos2 = np.concatenate([cos, cos], axis=1).astype(np.float32)
    sin2 = np.concatenate([-sin, sin], axis=1).astype(np.float32)
    return jnp.asarray(cos2), jnp.asarray(sin2)


def _head_layer_norm(t, g):
    mu = jnp.mean(t, axis=-1, keepdims=True)
    tc = t - mu
    var = jnp.mean(tc * tc, axis=-1, keepdims=True)
    return tc * lax.rsqrt(var + EPS) * g


def _retention_init(st_ref):
    st_ref[...] = jnp.zeros_like(st_ref)


def _retention_body(q_ref, k_ref, v_ref, g_ref, cos_ref, sin_ref, ng_ref, o_ref, st_ref):
    cos = cos_ref[...]
    sin = sin_ref[...]
    row = lax.broadcasted_iota(jnp.int32, (CHUNK, CHUNK), 0)
    col = lax.broadcasted_iota(jnp.int32, (CHUNK, CHUNK), 1)
    rel = (row - col).astype(F32)
    causal = row >= col
    pos = lax.broadcasted_iota(jnp.int32, (CHUNK, 1), 0).astype(F32)
    scale = RET_DIM ** -0.5
    for h in range(RET_HEADS):
        lg = _RET_LOG_GAMMA[h]
        sl = slice(h * RET_DIM, (h + 1) * RET_DIM)
        q = q_ref[:, sl].astype(F32)
        k = k_ref[:, sl].astype(F32)
        q = q * cos + pltpu.roll(q, RET_DIM // 2, 1) * sin
        k = (k * cos + pltpu.roll(k, RET_DIM // 2, 1) * sin) * scale
        qb = q.astype(BF16)
        kb = k.astype(BF16)
        vb = v_ref[:, sl].astype(BF16)
        decay = jnp.where(causal, jnp.exp(lg * jnp.maximum(rel, 0.0)), 0.0)
        scores = _dot_nt(qb, kb) * decay
        state = st_ref[h]
        out = _dot(scores.astype(BF16), vb)
        out = out + _dot(qb, state.astype(BF16)) * jnp.exp(lg * (pos + 1.0))
        kd = (k * jnp.exp(lg * (CHUNK - 1.0 - pos))).astype(BF16)
        st_ref[h] = math.exp(lg * CHUNK) * state + _dot_tn(kd, vb)
        y = _head_layer_norm(out, ng_ref[h:h + 1, :]) * _silu(g_ref[:, sl].astype(F32))
        o_ref[:, sl] = y.astype(o_ref.dtype)


class _MixerPart:
    def __init__(self, init, body, width, in_specs, args, scratch):
        self.init, self.body, self.width = init, body, width
        self.in_specs, self.args, self.scratch = in_specs, args, scratch


def _mixer_pair_kernel(*refs, parts):
    n_in = sum(len(p.args) for p in parts)
    o_ref = refs[n_in]
    ins, scr, lanes = [], [], []
    i0, s0, c0 = 0, n_in + 1, 0
    for p in parts:
        ins.append(refs[i0:i0 + len(p.args)])
        scr.append(refs[s0:s0 + len(p.scratch)])
        lanes.append((c0, c0 + p.width))
        i0, s0, c0 = i0 + len(p.args), s0 + len(p.scratch), c0 + p.width

    @pl.when(pl.program_id(1) == 0)
    def _():
        for p, s in zip(parts, scr):
            p.init(*s)

    for p, a, s, (lo, hi) in zip(parts, ins, scr, lanes):
        p.body(*a, o_ref.at[:, lo:hi], *s)


def _mixer_pair(parts, b, l, name):
    width = sum(p.width for p in parts)
    return pl.pallas_call(
        functools.partial(_mixer_pair_kernel, parts=parts),
        grid=(b, l // CHUNK),
        in_specs=[s for p in parts for s in p.in_specs],
        out_specs=pl.BlockSpec((None, CHUNK, width), lambda bi, c: (bi, c, 0)),
        out_shape=jax.ShapeDtypeStruct((b, l, width), BF16),
        scratch_shapes=[s for p in parts for s in p.scratch],
        compiler_params=_cparams(("parallel", "arbitrary")),
        name=name,
    )(*[a for p in parts for a in p.args])


def _retention(proj, norm_g, cos2, sin2):
    col_spec = lambda j: pl.BlockSpec((None, CHUNK, RET_WIDTH), lambda bi, c: (bi, c, j))
    tab_spec = pl.BlockSpec((CHUNK, RET_DIM), lambda bi, c: (c, 0))
    return _MixerPart(
        _retention_init, _retention_body, RET_WIDTH,
        [col_spec(0), col_spec(1), col_spec(2), col_spec(3), tab_spec, tab_spec,
         pl.BlockSpec((RET_HEADS, RET_DIM), lambda bi, c: (0, 0))],
        (proj, proj, proj, proj, cos2, sin2, norm_g),
        [pltpu.VMEM((RET_HEADS, RET_DIM, RET_DIM), F32)])


def _conv_reset(buf_ref):
    buf_ref[0:CONV_PAD, :] = jnp.zeros((CONV_PAD, buf_ref.shape[1]), F32)


def _conv_silu(x, buf_ref, w_ref, b_ref):
    n = x.shape[0]
    buf_ref[CONV_PAD:CONV_PAD + n, :] = x
    acc = b_ref[...] + w_ref[CONV_W - 1:CONV_W, :] * x
    for j in range(CONV_W - 1):
        off = CONV_PAD - (CONV_W - 1) + j
        acc = acc + w_ref[j:j + 1, :] * buf_ref[off:off + n, :]
    buf_ref[0:CONV_PAD, :] = x[n - CONV_PAD:, :]
    return _silu(acc)


def _ssd_init(st_ref, bufx_ref, bufb_ref):
    st_ref[...] = jnp.zeros_like(st_ref)
    _conv_reset(bufx_ref)
    _conv_reset(bufb_ref)


def _ssd_body(z_ref, xs_ref, bc_ref, dt_ref, cwx_ref, cbx_ref, cwb_ref, cbb_ref, dtb_ref, alog_ref,
              dskip_ref, ng_ref, o_ref, st_ref, bufx_ref, bufb_ref):
    xs = _conv_silu(xs_ref[...].astype(F32), bufx_ref, cwx_ref, cbx_ref)
    bc = _conv_silu(bc_ref[...].astype(F32), bufb_ref, cwb_ref, cbb_ref)
    dt = _softplus(dt_ref[...] + dtb_ref[...])
    da = dt * (-jnp.exp(alog_ref[...]))
    a_cum = _dot_exact(_tril_f32(CHUNK), da)
    a_cum_t = a_cum.T
    a_last = a_cum[CHUNK - 1:CHUNK, :]
    e_cum = jnp.exp(a_cum)
    to_end = jnp.exp(a_last - a_cum)
    e_last = jnp.exp(a_last)
    row = lax.broadcasted_iota(jnp.int32, (CHUNK, CHUNK), 0)
    col = lax.broadcasted_iota(jnp.int32, (CHUNK, CHUNK), 1)
    causal = row >= col
    gn = SSD_GROUPS * SSD_STATE
    for g in range(SSD_GROUPS):
        bm = bc[:, g * SSD_STATE:(g + 1) * SSD_STATE].astype(BF16)
        cm = bc[:, gn + g * SSD_STATE:gn + (g + 1) * SSD_STATE].astype(BF16)
        cb = _dot_nt(cm, bm)
        ys = []
        for r in range(SSD_HPG):
            hd = g * SSD_HPG + r
            sl = slice(hd * SSD_HEAD_DIM, (hd + 1) * SSD_HEAD_DIM)
            x_h = xs[:, sl]
            xdt = x_h * dt[:, hd:hd + 1]
            seg = jnp.exp(jnp.where(causal, a_cum[:, hd:hd + 1] - a_cum_t[hd:hd + 1, :], -jnp.inf))
            state = st_ref[hd]
            y = _dot((cb * seg).astype(BF16), xdt.astype(BF16))
            y = y + _dot(cm, state.astype(BF16)) * e_cum[:, hd:hd + 1]
            st_ref[hd] = e_last[:, hd:hd + 1] * state + _dot_tn(bm, (xdt * to_end[:, hd:hd + 1]).astype(BF16))
            ys.append(y + dskip_ref[:, sl] * x_h)
        gsl = slice(g * SSD_GW, (g + 1) * SSD_GW)
        y = jnp.concatenate(ys, axis=1) * _silu(z_ref[:, gsl].astype(F32))
        y = y * lax.rsqrt(jnp.mean(y * y, axis=-1, keepdims=True) + EPS)
        o_ref[:, gsl] = (y * ng_ref[:, gsl]).astype(o_ref.dtype)


def _ssd(proj, gates, conv_w, conv_b, dt_bias, a_log, d_skip, norm_g):
    bcw = 2 * SSD_GROUPS * SSD_STATE
    pad = lambda v: jnp.pad(v.astype(F32), (0, LANES - v.shape[0]))[None, :]
    full = lambda shape: pl.BlockSpec(shape, lambda bi, c: (0,) * len(shape))
    z_off = 4 * RET_WIDTH // SSD_WIDTH
    return _MixerPart(
        _ssd_init, _ssd_body, SSD_WIDTH,
        [
            pl.BlockSpec((None, CHUNK, SSD_WIDTH), lambda bi, c: (bi, c, z_off)),
            pl.BlockSpec((None, CHUNK, SSD_WIDTH), lambda bi, c: (bi, c, z_off + 1)),
            pl.BlockSpec((None, CHUNK, bcw), lambda bi, c: (bi, c, (z_off + 2) * SSD_WIDTH // bcw)),
            pl.BlockSpec((None, CHUNK, LANES), lambda bi, c: (bi, c, 0)),
            full((CONV_W, SSD_WIDTH)), full((1, SSD_WIDTH)), full((CONV_W, bcw)), full((1, bcw)),
            full((1, LANES)), full((1, LANES)), full((1, SSD_WIDTH)), full((1, SSD_WIDTH)),
        ],
        (proj, proj, proj, gates,
         conv_w[:, :SSD_WIDTH], conv_b[None, :SSD_WIDTH], conv_w[:, SSD_WIDTH:], conv_b[None, SSD_WIDTH:],
         pad(dt_bias), pad(a_log), jnp.repeat(d_skip.astype(F32), SSD_HEAD_DIM)[None, :], norm_g[None, :]),
        [
            pltpu.VMEM((SSD_HEADS, SSD_STATE, SSD_HEAD_DIM), F32),
            pltpu.VMEM((CONV_PAD + CHUNK, SSD_WIDTH), F32),
            pltpu.VMEM((CONV_PAD + CHUNK, bcw), F32),
        ])


def _mlstm_init(c_ref, n_ref, m_ref, bufq_ref, bufk_ref):
    c_ref[...] = jnp.zeros_like(c_ref)
    n_ref[...] = jnp.zeros_like(n_ref)
    m_ref[...] = jnp.full(m_ref.shape, -jnp.inf, F32)
    _conv_reset(bufq_ref)
    _conv_reset(bufk_ref)


def _mlstm_body(q_ref, k_ref, v_ref, o_ref, gt_ref, cwq_ref, cbq_ref, cwk_ref, cbk_ref, gb_ref, ng_ref,
                out_ref, c_ref, n_ref, m_ref, bufq_ref, bufk_ref):
    nh = MLSTM_HEADS
    pre = gt_ref[...] + gb_ref[...]
    bcum_all = _dot_exact(_tril_f32(CHUNK), -_softplus(-pre))
    pre_t = pre.T
    bcum_all_t = bcum_all.T
    row = lax.broadcasted_iota(jnp.int32, (CHUNK, CHUNK), 0)
    col = lax.broadcasted_iota(jnp.int32, (CHUNK, CHUNK), 1)
    causal = row >= col

    for hd in range(nh):
        sl = slice(hd * MLSTM_DIM, (hd + 1) * MLSTM_DIM)
        q = _conv_silu(q_ref[:, sl].astype(F32), bufq_ref.at[:, sl], cwq_ref.at[:, sl], cbq_ref.at[:, sl])
        q = q * (MLSTM_DIM ** -0.5)
        k = _conv_silu(k_ref[:, sl].astype(F32), bufk_ref.at[:, sl], cwk_ref.at[:, sl], cbk_ref.at[:, sl])
        qb = q.astype(BF16)
        kb = k.astype(BF16)
        vb = v_ref[:, sl].astype(BF16)
        ig = pre[:, hd:hd + 1]
        ig_t = pre_t[hd:hd + 1, :]
        bcum = bcum_all[:, nh + hd:nh + hd + 1]
        bcum_t = bcum_all_t[nh + hd:nh + hd + 1, :]

        log_d = jnp.where(causal, bcum - bcum_t + ig_t, -jnp.inf)
        m_prev = m_ref[hd]
        m_inter = bcum + m_prev
        m_t = jnp.maximum(jnp.max(log_d, axis=1, keepdims=True), m_inter)
        scores = _dot_nt(qb, kb) * jnp.exp(log_d - m_t)
        inter_scale = jnp.exp(m_inter - m_t)
        c_st = c_ref[hd]
        n_st = n_ref[hd]
        num = _dot(scores.astype(BF16), vb) + inter_scale * _dot(qb, c_st.astype(BF16))
        den = jnp.sum(scores, axis=1, keepdims=True) + inter_scale * jnp.sum(q * n_st, axis=1, keepdims=True)
        hval = num / jnp.maximum(jnp.abs(den), jnp.exp(-m_t))

        b_last = bcum[CHUNK - 1:CHUNK, :]
        w_log = b_last - bcum + ig
        m_new = jnp.maximum(b_last + m_prev, jnp.max(w_log, axis=0, keepdims=True))
        kw = k * jnp.exp(w_log - m_new)
        prev_scale = jnp.exp(b_last + m_prev - m_new)
        c_ref[hd] = prev_scale * c_st + _dot_tn(kw.astype(BF16), vb)
        n_ref[hd] = prev_scale * n_st + jnp.sum(kw, axis=0, keepdims=True)
        m_ref[hd] = m_new

        y = _head_layer_norm(hval, ng_ref[:, sl]) * _sigmoid(o_ref[:, sl].astype(F32))
        out_ref[:, sl] = y.astype(out_ref.dtype)


def _mlstm(proj, gates, conv_w, conv_b, i_bias, f_bias, norm_g):
    nh = MLSTM_HEADS
    w = MLSTM_WIDTH
    col_spec = lambda j: pl.BlockSpec((None, CHUNK, w), lambda bi, c: (bi, c, j))
    cw_spec = lambda j: pl.BlockSpec((CONV_W, w), lambda bi, c: (0, j))
    cb_spec = lambda j: pl.BlockSpec((1, w), lambda bi, c: (0, j))
    gate_bias = jnp.pad(jnp.concatenate([i_bias, f_bias]).astype(F32), (0, LANES - 2 * nh))[None, :]
    return _MixerPart(
        _mlstm_init, _mlstm_body, w,
        [
            col_spec(0), col_spec(1), col_spec(2), col_spec(3),
            pl.BlockSpec((None, CHUNK, LANES), lambda bi, c: (bi, c, 0)),
            cw_spec(0), cb_spec(0), cw_spec(1), cb_spec(1),
            pl.BlockSpec((1, LANES), lambda bi, c: (0, 0)),
            pl.BlockSpec((1, w), lambda bi, c: (0, 0)),
        ],
        (proj, proj, proj, proj, gates, conv_w, conv_b[None, :], conv_w, conv_b[None, :], gate_bias,
         norm_g.reshape(1, w)),
        [
            pltpu.VMEM((nh, MLSTM_DIM, MLSTM_DIM), F32),
            pltpu.VMEM((nh, 1, MLSTM_DIM), F32),
            pltpu.VMEM((nh, 1, 1), F32),
            pltpu.VMEM((CONV_PAD + CHUNK, w), F32),
            pltpu.VMEM((CONV_PAD + CHUNK, w), F32),
        ])


S5_SEG = CHUNK // CONV_PAD
S5_SLAB = 128
S5_SLAB_LANES = S5_SLAB // S5_GROUP * S5_STATE
_S5_SEG_LOG = int(math.log2(CONV_PAD))


def _cmul(a_re, a_im, b_re, b_im):
    return a_re * b_re - a_im * b_im, a_re * b_im + a_im * b_re


def _s5_init(sre_ref, sim_ref, st_re_ref, st_im_ref, ucol_ref):
    st_re_ref[...] = jnp.zeros_like(st_re_ref)
    st_im_ref[...] = jnp.zeros_like(st_im_ref)


def _s5_body(u_ref, bre_ref, bim_ref, pt_re_ref, pt_im_ref, sg_re_ref, sg_im_ref, cre_ref, cim_ref,
             dskip_ref, wglu_ref, bglu_ref, o_ref, sre_ref, sim_ref, st_re_ref, st_im_ref, ucol_ref):
    for q in range(S5_WIDTH // LANES):
        ucol_ref[q] = u_ref[:, q * LANES:(q + 1) * LANES].astype(F32)
    u = jnp.concatenate(
        [jnp.concatenate([ucol_ref[q, pl.ds(v, CONV_PAD, stride=S5_SEG), :] for v in range(S5_SEG)], axis=0)
         for q in range(S5_WIDTH // LANES)], axis=1)
    ub = u.astype(BF16)
    sub = lax.broadcasted_iota(jnp.int32, (CONV_PAD, LANES), 0)
    grp = lambda x, v: x[v * CONV_PAD:(v + 1) * CONV_PAD, :]
    for jj in range(S5_WIDTH // S5_SLAB):
        ch = slice(jj * S5_SLAB, (jj + 1) * S5_SLAB)
        xr = _dot(ub[:, ch], bre_ref[ch, jj * S5_SLAB_LANES:(jj + 1) * S5_SLAB_LANES])
        xi = _dot(ub[:, ch], bim_ref[ch, jj * S5_SLAB_LANES:(jj + 1) * S5_SLAB_LANES])
        for q in range(S5_SLAB_LANES // LANES):
            sl = slice(jj * S5_SLAB_LANES + q * LANES, jj * S5_SLAB_LANES + (q + 1) * LANES)
            lre = xr[:, q * LANES:(q + 1) * LANES]
            lim = xi[:, q * LANES:(q + 1) * LANES]
            a_re = pt_re_ref[0:CONV_PAD, sl]
            a_im = pt_im_ref[0:CONV_PAD, sl]
            s_re = [grp(lre, 0)]
            s_im = [grp(lim, 0)]
            for v in range(1, S5_SEG):
                d_re, d_im = _cmul(a_re, a_im, s_re[-1], s_im[-1])
                s_re.append(grp(lre, v) + d_re)
                s_im.append(grp(lim, v) + d_im)
            c_re = st_re_ref[:, sl]
            c_im = st_im_ref[:, sl]
            i_re, i_im = _cmul(sg_re_ref[0:1, sl], sg_im_ref[0:1, sl], c_re, c_im)
            e_re = s_re[-1] + jnp.where(sub == 0, i_re, 0.0)
            e_im = s_im[-1] + jnp.where(sub == 0, i_im, 0.0)
            for k in range(_S5_SEG_LOG):
                d = 1 << k
                h_re = jnp.where(sub >= d, pltpu.roll(e_re, d, 0), 0.0)
                h_im = jnp.where(sub >= d, pltpu.roll(e_im, d, 0), 0.0)
                d_re, d_im = _cmul(sg_re_ref[k:k + 1, sl], sg_im_ref[k:k + 1, sl], h_re, h_im)
                e_re = e_re + d_re
                e_im = e_im + d_im
            st_re_ref[:, sl] = e_re[CONV_PAD - 1:CONV_PAD, :]
            st_im_ref[:, sl] = e_im[CONV_PAD - 1:CONV_PAD, :]
            p_re = jnp.where(sub >= 1, pltpu.roll(e_re, 1, 0), c_re)
            p_im = jnp.where(sub >= 1, pltpu.roll(e_im, 1, 0), c_im)
            for v in range(S5_SEG):
                rows = slice(v * CONV_PAD, (v + 1) * CONV_PAD)
                d_re, d_im = _cmul(pt_re_ref[rows, sl], pt_im_ref[rows, sl], p_re, p_im)
                s_re[v] = s_re[v] + d_re
                s_im[v] = s_im[v] + d_im
            sre_ref[:, sl] = jnp.concatenate(s_re, axis=0).astype(BF16)
            sim_ref[:, sl] = jnp.concatenate(s_im, axis=0).astype(BF16)

    ys = []
    for jj in range(S5_WIDTH // S5_SLAB):
        ch = slice(jj * S5_SLAB, (jj + 1) * S5_SLAB)
        ln = slice(jj * S5_SLAB_LANES, (jj + 1) * S5_SLAB_LANES)
        ys.append(_dot(sre_ref[:, ln], cre_ref[ln, ch]) - _dot(sim_ref[:, ln], cim_ref[ln, ch]))
    y = jnp.concatenate(ys, axis=1) + dskip_ref[...] * u
    y = 0.5 * y * (1.0 + jnp.tanh(math.sqrt(2.0 / math.pi) * (y + 0.044715 * (y * y * y))))
    vg = _dot(y.astype(BF16), wglu_ref[...]) + bglu_ref[...]
    res = (vg[:, :S5_WIDTH] * _sigmoid(vg[:, S5_WIDTH:])).astype(BF16)
    t_idx = lax.broadcasted_iota(jnp.int32, (CHUNK, CHUNK), 0)
    p_idx = lax.broadcasted_iota(jnp.int32, (CHUNK, CHUNK), 1)
    unperm = jnp.where(p_idx == CONV_PAD * (t_idx % S5_SEG) + t_idx // S5_SEG, 1.0, 0.0).astype(BF16)
    o_ref[...] = _dot(unperm, res).astype(o_ref.dtype)


def _s5_params(a_re, a_im, log_dt, b_re, b_im, c_re, c_im):
    a_re = a_re.astype(F32)
    a_im = a_im.astype(F32)
    dt = jnp.exp(log_dt.astype(F32))[:, None]
    mag = jnp.exp(a_re * dt)
    abar_re = mag * jnp.cos(a_im * dt)
    abar_im = mag * jnp.sin(a_im * dt)
    den = a_re * a_re + a_im * a_im
    nr = abar_re - 1.0
    coef_re = (nr * a_re + abar_im * a_im) / den
    coef_im = (abar_im * a_re - nr * a_im) / den
    b_re = b_re.astype(F32)
    b_im = b_im.astype(F32)
    bbar_re = coef_re[..., None] * b_re - coef_im[..., None] * b_im
    bbar_im = coef_re[..., None] * b_im + coef_im[..., None] * b_re
    eye = jnp.eye(S5_GROUPS, dtype=F32)
    in_map = lambda m: jnp.einsum('gnc,gh->gchn', m, eye).reshape(S5_WIDTH, S5_LANES).astype(BF16)
    out_map = lambda m: jnp.einsum('gcn,gh->gnhc', m.astype(F32), eye).reshape(S5_LANES, S5_WIDTH).astype(BF16)

    def powers(exps):
        e = exps[:, None, None]
        m = jnp.exp(e * (a_re * dt)[None])
        ang = e * (a_im * dt)[None]
        return (m * jnp.cos(ang)).reshape(-1, S5_LANES), (m * jnp.sin(ang)).reshape(-1, S5_LANES)

    pt_re, pt_im = powers(jnp.repeat(jnp.arange(1, S5_SEG + 1, dtype=F32), CONV_PAD))
    sg_re, sg_im = powers(jnp.asarray([float(S5_SEG << k) for k in range(_S5_SEG_LOG)], F32))
    pad8 = lambda m: jnp.pad(m, ((0, CONV_PAD - m.shape[0]), (0, 0)))
    return (in_map(bbar_re), in_map(bbar_im), pt_re, pt_im, pad8(sg_re), pad8(sg_im),
            out_map(c_re), out_map(c_im))


def _s5(proj, a_re, a_im, log_dt, b_re, b_im, c_re, c_im, d_skip, w_glu, b_glu):
    tabs = _s5_params(a_re, a_im, log_dt, b_re, b_im, c_re, c_im)
    full = lambda shape: pl.BlockSpec(shape, lambda bi, c: (0,) * len(shape))
    u_off = 4 * MLSTM_WIDTH // S5_WIDTH
    return _MixerPart(
        _s5_init, _s5_body, S5_WIDTH,
        [
            pl.BlockSpec((None, CHUNK, S5_WIDTH), lambda bi, c: (bi, c, u_off)),
            full((S5_WIDTH, S5_LANES)), full((S5_WIDTH, S5_LANES)),
            full((CHUNK, S5_LANES)), full((CHUNK, S5_LANES)),
            full((CONV_PAD, S5_LANES)), full((CONV_PAD, S5_LANES)),
            full((S5_LANES, S5_WIDTH)), full((S5_LANES, S5_WIDTH)),
            full((1, S5_WIDTH)), full((S5_WIDTH, 2 * S5_WIDTH)), full((1, 2 * S5_WIDTH)),
        ],
        (proj, *tabs, d_skip.astype(F32).reshape(1, S5_WIDTH), w_glu.astype(BF16), b_glu.astype(F32)[None, :]),
        [
            pltpu.VMEM((CHUNK, S5_LANES), BF16), pltpu.VMEM((CHUNK, S5_LANES), BF16),
            pltpu.VMEM((1, S5_LANES), F32), pltpu.VMEM((1, S5_LANES), F32),
            pltpu.VMEM((S5_WIDTH // LANES, CHUNK, LANES), F32),
        ])


def _router_kernel(x_ref, g_ref, whi_ref, wlo_ref, b_ref, ids_ref, wts_ref, cnt_ref, carry_ref):
    @pl.when(pl.program_id(0) == 0)
    def _():
        carry_ref[...] = jnp.zeros_like(carry_ref)

    x = x_ref[...]
    xn = x * lax.rsqrt(jnp.mean(x * x, axis=-1, keepdims=True) + EPS) * g_ref[...]
    xh = xn.astype(BF16)
    xl = (xn - xh.astype(F32)).astype(BF16)
    logits = _dot(xh, whi_ref[...]) + (_dot(xl, whi_ref[...]) + _dot(xh, wlo_ref[...])) + b_ref[...]
    lane = lax.broadcasted_iota(jnp.int32, logits.shape, 1)
    big = jnp.int32(LANES)
    neg = -jnp.inf

    def top(vals):
        m = jnp.max(vals, axis=1, keepdims=True)
        idx = jnp.min(jnp.where(vals == m, lane, big), axis=1, keepdims=True)
        return m, idx

    gl = jnp.where(lane < MOE_GROUPS, logits, neg)
    g_max, g_idx = top(gl)
    g_w = 1.0 / jnp.sum(jnp.exp(gl - g_max), axis=1, keepdims=True)
    lo = MOE_GROUPS + EXPERTS_PER_GROUP * g_idx
    el = jnp.where((lane >= lo) & (lane < lo + EXPERTS_PER_GROUP), logits, neg)
    m1, j1 = top(el)
    m2, j2 = top(jnp.where(lane == j1, neg, el))
    e2 = jnp.exp(m2 - m1)
    w1 = g_w / (1.0 + e2)
    w2 = g_w * e2 / (1.0 + e2)
    e1 = j1 - MOE_GROUPS
    e2 = j2 - MOE_GROUPS

    tm = x.shape[0]
    hit = jnp.where(lane == e1, 1.0, jnp.where(lane == e2, 1.0, 0.0))
    row = lax.broadcasted_iota(jnp.int32, (tm, tm), 0)
    col = lax.broadcasted_iota(jnp.int32, (tm, tm), 1)
    before = jnp.where(row > col, 1.0, 0.0).astype(BF16)
    prefix = _dot(before, hit.astype(BF16)) + carry_ref[...]
    r1 = jnp.sum(jnp.where(lane == e1, prefix, 0.0), axis=1, keepdims=True)
    r2 = jnp.sum(jnp.where(lane == e2, prefix, 0.0), axis=1, keepdims=True)
    carry_ref[...] = carry_ref[...] + jnp.sum(hit, axis=0, keepdims=True)
    cnt_ref[...] = carry_ref[...]

    wts_ref[...] = jnp.where(lane == 0, w1, jnp.where(lane == 1, w2, 0.0))
    ids = jnp.where(lane == 0, e1.astype(F32), jnp.where(lane == 1, e2.astype(F32),
                                                         jnp.where(lane == 2, r1, jnp.where(lane == 3, r2, 0.0))))
    for c in range(tm // LANES):
        ids_ref[:, c * LANES:(c + 1) * LANES] = ids[c * LANES:(c + 1) * LANES, :].T[0:CONV_PAD, :].astype(jnp.int32)


def _router(x, g, w, bias, tm=512):
    t, d = x.shape
    w_hi = w.astype(BF16)
    w_lo = (w - w_hi.astype(F32)).astype(BF16)
    return pl.pallas_call(
        _router_kernel,
        grid=(t // tm,),
        in_specs=[
            pl.BlockSpec((tm, d), lambda i: (i, 0)),
            pl.BlockSpec((1, d), lambda i: (0, 0)),
            pl.BlockSpec((d, LANES), lambda i: (0, 0)),
            pl.BlockSpec((d, LANES), lambda i: (0, 0)),
            pl.BlockSpec((1, LANES), lambda i: (0, 0)),
        ],
        out_specs=[pl.BlockSpec((CONV_PAD, tm), lambda i: (0, i)), pl.BlockSpec((tm, LANES), lambda i: (i, 0)),
                   pl.BlockSpec((1, LANES), lambda i: (0, 0))],
        out_shape=[jax.ShapeDtypeStruct((CONV_PAD, t), jnp.int32), jax.ShapeDtypeStruct((t, LANES), F32),
                   jax.ShapeDtypeStruct((1, LANES), F32)],
        scratch_shapes=[pltpu.VMEM((1, LANES), F32)],
        compiler_params=_cparams(("arbitrary",)),
        name="moe_router",
    )(x, g, w_hi, w_lo, bias)


def _row_gather_copy(src_hbm, idx_ref, base, buf_ref, slot, sem_ref, r):
    return pltpu.make_async_copy(src_hbm.at[pl.ds(idx_ref[base + r], 1)], buf_ref.at[slot, pl.ds(r, 1)],
                                 sem_ref.at[slot])


def _start_row_gather(src_hbm, idx_ref, base, buf_ref, slot, sem_ref, rows):
    def body(r, carry):
        _row_gather_copy(src_hbm, idx_ref, base, buf_ref, slot, sem_ref, r).start()
        return carry

    lax.fori_loop(0, rows, body, 0)


def _start_row_gather_inline(src_hbm, idx_ref, base, buf_ref, slot, sem_ref, rows, alternate):
    for r in range(rows):
        _row_gather_copy(src_hbm, idx_ref, base, buf_ref, slot, sem_ref, r).start(priority=r % 2 if alternate else 0)


def _wait_row_gather(src_hbm, buf_ref, slot, sem_ref, rows):
    pltpu.make_async_copy(src_hbm.at[pl.ds(0, rows)], buf_ref.at[slot], sem_ref.at[slot]).wait()


def _dispatch_kernel(d0_ref, d1_ref, padrow_ref, padcnt_ref, x_ref, xs_hbm, sem_ref, zsem_ref, zero_ref):
    i = pl.program_id(0)
    rows = x_ref.shape[0]
    base = i * rows

    def zero_copy(row):
        return pltpu.make_async_copy(zero_ref.at[pl.ds(0, 1)], xs_hbm.at[pl.ds(row, 1)], zsem_ref.at[0])

    def for_pad_rows(fn):
        def per_expert(e, carry):
            first = padrow_ref[e]

            def per_row(j, c):
                fn(first + j)
                return c

            lax.fori_loop(0, padcnt_ref[e], per_row, 0)
            return carry

        lax.fori_loop(0, N_EXPERTS, per_expert, 0)

    def tail_copy(j):
        row = pl.multiple_of(padrow_ref[N_EXPERTS] + j * CONV_PAD, CONV_PAD)
        return pltpu.make_async_copy(zero_ref, xs_hbm.at[pl.ds(row, CONV_PAD)], zsem_ref.at[1])

    def for_tail(fn):
        def per_group(j, carry):
            fn(j)
            return carry

        lax.fori_loop(0, padcnt_ref[N_EXPERTS] // CONV_PAD, per_group, 0)

    @pl.when(i == 0)
    def _():
        zero_ref[...] = jnp.zeros_like(zero_ref)
        for_pad_rows(lambda row: zero_copy(row).start())
        for_tail(lambda j: tail_copy(j).start())

    def body(r, carry):
        src = x_ref.at[pl.ds(r, 1)]
        pltpu.make_async_copy(src, xs_hbm.at[pl.ds(d0_ref[base + r], 1)], sem_ref.at[0]).start()
        pltpu.make_async_copy(src, xs_hbm.at[pl.ds(d1_ref[base + r], 1)], sem_ref.at[1]).start(priority=1)
        return carry

    lax.fori_loop(0, rows, body, 0, unroll=8)
    for k in range(MOE_TOP_K):
        pltpu.make_async_copy(x_ref, xs_hbm.at[pl.ds(0, rows)], sem_ref.at[k]).wait()

    @pl.when(i == pl.num_programs(0) - 1)
    def _():
        for_pad_rows(lambda row: zero_copy(row).wait())
        for_tail(lambda j: tail_copy(j).wait())


def _dispatch(x, dest0, dest1, pad_row, pad_cnt, n_rows, tm=512):
    t, w = x.shape
    grid_spec = pltpu.PrefetchScalarGridSpec(
        num_scalar_prefetch=4,
        grid=(t // tm,),
        in_specs=[pl.BlockSpec((tm, w), lambda i, *_: (i, 0))],
        out_specs=pl.BlockSpec(memory_space=pl.ANY),
        scratch_shapes=[pltpu.SemaphoreType.DMA((MOE_TOP_K,)), pltpu.SemaphoreType.DMA((2,)),
                        pltpu.VMEM((CONV_PAD, w), x.dtype)],
    )
    return pl.pallas_call(
        _dispatch_kernel,
        grid_spec=grid_spec,
        out_shape=jax.ShapeDtypeStruct((n_rows, w), x.dtype),
        compiler_params=_cparams(("arbitrary",)),
        name="moe_dispatch",
    )(dest0, dest1, pad_row, pad_cnt, x)


def _expert_kernel(be_ref, first_ref, par_ref, nxt_ref, nused_ref, x_ref, g_ref, wg_hbm, wu_hbm, wd_hbm,
                   o_ref, wgf_ref, wuf_ref, wdf_ref, wsem_ref, wgb_ref, wub_ref, wdb_ref, *, layer):
    i = pl.program_id(0)
    n_used = nused_ref[0]

    def weight_copies(e, p):
        return (pltpu.make_async_copy(wg_hbm.at[layer, e], wgf_ref.at[p], wsem_ref.at[0, p]),
                pltpu.make_async_copy(wu_hbm.at[layer, e], wuf_ref.at[p], wsem_ref.at[1, p]),
                pltpu.make_async_copy(wd_hbm.at[layer, e], wdf_ref.at[p], wsem_ref.at[2, p]))

    @pl.when(i == 0)
    def _():
        for c in weight_copies(be_ref[0], 0):
            c.start(priority=1)

    @pl.when((i < n_used) & (first_ref[i] == 1))
    def _():
        p = par_ref[i]
        for c in weight_copies(be_ref[i], p):
            c.wait()
        nxt = nxt_ref[i]

        @pl.when(nxt >= 0)
        def _():
            for c in weight_copies(nxt, 1 - p):
                c.start(priority=1)

        wgb_ref[...] = wgf_ref[p].astype(BF16)
        wub_ref[...] = wuf_ref[p].astype(BF16)
        wdb_ref[...] = wdf_ref[p].astype(BF16)

    @pl.when(i < n_used)
    def _():
        x = x_ref[...]
        xn = (x * lax.rsqrt(jnp.mean(x * x, axis=-1, keepdims=True) + EPS) * g_ref[...]).astype(BF16)
        act = (_silu(_dot(xn, wgb_ref[...])) * _dot(xn, wub_ref[...])).astype(BF16)
        o_ref[...] = _dot(act, wdb_ref[...])

    @pl.when(i >= n_used)
    def _():
        o_ref[...] = jnp.zeros_like(o_ref)


def _expert_ffn(x_sorted, g, tables, n_used, w_gate, w_up, w_down, layer):
    n_rows, d = x_sorted.shape
    n_blocks = n_rows // MOE_ROWS
    ff = w_gate.shape[-1]
    any_spec = pl.BlockSpec(memory_space=pl.ANY)
    grid_spec = pltpu.PrefetchScalarGridSpec(
        num_scalar_prefetch=5,
        grid=(n_blocks,),
        in_specs=[pl.BlockSpec((MOE_ROWS, d), lambda i, be, fi, pa, nx, nu: (jnp.minimum(i, nu[0] - 1), 0)),
                  pl.BlockSpec((1, d), lambda i, *_: (0, 0)), any_spec, any_spec, any_spec],
        out_specs=pl.BlockSpec((MOE_ROWS, d), lambda i, *_: (i, 0)),
        scratch_shapes=[
            pltpu.VMEM((2, d, ff), F32), pltpu.VMEM((2, d, ff), F32), pltpu.VMEM((2, ff, d), F32),
            pltpu.SemaphoreType.DMA((3, 2)),
            pltpu.VMEM((d, ff), BF16), pltpu.VMEM((d, ff), BF16), pltpu.VMEM((ff, d), BF16),
        ],
    )
    return pl.pallas_call(
        functools.partial(_expert_kernel, layer=layer),
        grid_spec=grid_spec,
        out_shape=jax.ShapeDtypeStruct((n_rows, d), F32),
        compiler_params=_cparams(("arbitrary",)),
        name="moe_experts",
    )(*tables, n_used, x_sorted, g, w_gate, w_up, w_down)


def _combine_kernel(d0_ref, d1_ref, y_hbm, h_ref, w_ref, fg_ref, o_ref, buf0_ref, buf1_ref, sem0_ref, sem1_ref, *,
                    final_norm):
    i = pl.program_id(0)
    n = pl.num_programs(0)
    slot = lax.rem(i, 2)
    next_base = lax.rem(i + 1, n) * COMBINE_ROWS

    @pl.when(i == 0)
    def _():
        _start_row_gather(y_hbm, d0_ref, 0, buf0_ref, 0, sem0_ref, COMBINE_ROWS)
        _start_row_gather(y_hbm, d1_ref, 0, buf1_ref, 0, sem1_ref, COMBINE_ROWS)

    _wait_row_gather(y_hbm, buf0_ref, slot, sem0_ref, COMBINE_ROWS)
    _wait_row_gather(y_hbm, buf1_ref, slot, sem1_ref, COMBINE_ROWS)
    _start_row_gather_inline(y_hbm, d0_ref, next_base, buf0_ref, 1 - slot, sem0_ref, COMBINE_ROWS, True)
    _start_row_gather_inline(y_hbm, d1_ref, next_base, buf1_ref, 1 - slot, sem1_ref, COMBINE_ROWS, True)
    w = w_ref[...]
    y = h_ref[...] + (w[:, 0:1] * buf0_ref[slot] + w[:, 1:2] * buf1_ref[slot])
    if final_norm:
        y = y * lax.rsqrt(jnp.mean(y * y, axis=-1, keepdims=True) + EPS) * fg_ref[...]
    o_ref[...] = y

    @pl.when(i == n - 1)
    def _():
        _wait_row_gather(y_hbm, buf0_ref, 1 - slot, sem0_ref, COMBINE_ROWS)
        _wait_row_gather(y_hbm, buf1_ref, 1 - slot, sem1_ref, COMBINE_ROWS)


def _combine(y_rows, dest0, dest1, h, wts, final_g, final_norm):
    t, d = h.shape
    grid_spec = pltpu.PrefetchScalarGridSpec(
        num_scalar_prefetch=2,
        grid=(t // COMBINE_ROWS,),
        in_specs=[
            pl.BlockSpec(memory_space=pl.ANY),
            pl.BlockSpec((COMBINE_ROWS, d), lambda i, d0, d1: (i, 0)),
            pl.BlockSpec((COMBINE_ROWS, LANES), lambda i, d0, d1: (i, 0)),
            pl.BlockSpec((1, d), lambda i, d0, d1: (0, 0)),
        ],
        out_specs=pl.BlockSpec((COMBINE_ROWS, d), lambda i, d0, d1: (i, 0)),
        scratch_shapes=[
            pltpu.VMEM((2, COMBINE_ROWS, d), F32), pltpu.VMEM((2, COMBINE_ROWS, d), F32),
            pltpu.SemaphoreType.DMA((2,)), pltpu.SemaphoreType.DMA((2,)),
        ],
    )
    return pl.pallas_call(
        functools.partial(_combine_kernel, final_norm=final_norm),
        grid_spec=grid_spec,
        out_shape=jax.ShapeDtypeStruct((t, d), F32),
        compiler_params=_cparams(("arbitrary",)),
        name="moe_combine",
    )(dest0, dest1, y_rows, h, wts, final_g)


def _moe(h, layer, norm_g, router_g, router_g_b, router_e, router_e_b, w_gate, w_up, w_down, final_g, final_norm):
    t, d = h.shape
    g = norm_g.astype(F32)[None, :]
    pad_cols = LANES - MOE_GROUPS - N_EXPERTS
    w_r = jnp.pad(jnp.concatenate([router_g, router_e], axis=1).astype(F32), ((0, 0), (0, pad_cols)))
    b_r = jnp.pad(jnp.concatenate([router_g_b, router_e_b]).astype(F32), (0, pad_cols))[None, :]
    ids, wts, cnt = _router(h, g, w_r, b_r)

    n_slots = t * MOE_TOP_K
    n_blocks = (n_slots + N_EXPERTS * (MOE_ROWS - 1) + MOE_ROWS - 1) // MOE_ROWS
    n_rows = n_blocks * MOE_ROWS
    counts = cnt[0, :N_EXPERTS].astype(jnp.int32)
    blocks_of = (counts + MOE_ROWS - 1) // MOE_ROWS
    blk_end = jnp.cumsum(blocks_of)
    blk_start = blk_end - blocks_of
    n_used = blk_end[-1:]
    blk = jnp.arange(n_blocks, dtype=jnp.int32)
    block_expert = jnp.minimum(jnp.sum((blk_end[None, :] <= blk[:, None]).astype(jnp.int32), axis=1), N_EXPERTS - 1)
    first = (blk == blk_start[block_expert]).astype(jnp.int32)
    parity = (jnp.cumsum(first) - 1) % 2
    nxt_blk = blk_end[block_expert]
    nxt_expert = jnp.where(nxt_blk < n_used[0], block_expert[jnp.minimum(nxt_blk, n_blocks - 1)], -1)
    tables = (block_expert, first, parity.astype(jnp.int32), nxt_expert.astype(jnp.int32))

    experts = jnp.arange(N_EXPERTS, dtype=jnp.int32)[:, None, None]
    first_row = jnp.sum(jnp.where(ids[None, 0:MOE_TOP_K] == experts, (blk_start * MOE_ROWS)[:, None, None], 0), axis=0)
    dest = first_row + ids[MOE_TOP_K:2 * MOE_TOP_K]

    pad_row = jnp.concatenate([blk_start * MOE_ROWS + counts, n_used * MOE_ROWS])
    pad_cnt = jnp.concatenate([blocks_of * MOE_ROWS - counts, n_rows - n_used * MOE_ROWS])
    x_sorted = _dispatch(h, dest[0], dest[1], pad_row, pad_cnt, n_rows)
    y_rows = _expert_ffn(x_sorted, g, tables, n_used, w_gate, w_up, w_down, layer)
    return _combine(y_rows, dest[0], dest[1], h, wts, final_g.astype(F32)[None, :], final_norm)


def _split_in_proj(w_in, gate_lo, gate_hi):
    main = jnp.concatenate([w_in[:, :gate_lo], w_in[:, gate_hi:]], axis=1).astype(BF16)
    small = jnp.pad(w_in[:, gate_lo:gate_hi], ((0, 0), (0, LANES - (gate_hi - gate_lo)))).astype(BF16)
    return main, small


def _even_layer(h, b, l, norm_g, w_in, ret_norm, conv_w, conv_b, dt_bias, a_log, d_skip, ssd_norm, w_out):
    w_main, w_small = _split_in_proj(w_in, MAIN_WIDTH, MAIN_WIDTH + SSD_HEADS)
    proj, gates = _norm_matmul(h, norm_g.astype(F32)[None, :], w_main, w_small)
    proj = proj.reshape(b, l, MAIN_WIDTH)
    gates = gates.reshape(b, l, LANES)
    cos2, sin2 = _rope_tables(l)
    mixed = _mixer_pair(
        [_retention(proj, ret_norm.astype(F32), cos2, sin2),
         _ssd(proj, gates, conv_w.astype(F32), conv_b.astype(F32), dt_bias, a_log, d_skip, ssd_norm.astype(F32))],
        b, l, "retention_ssd")
    return _out_proj(mixed.reshape(b * l, -1), w_out.astype(BF16), h)


def _odd_layer(h, b, l, norm_g, w_in, conv_w, conv_b, i_bias, f_bias, mlstm_norm, a_re, a_im, log_dt, b_re, b_im,
               c_re, c_im, d_skip, w_glu, b_glu, w_out):
    gate_lo = 4 * MLSTM_WIDTH
    w_main, w_small = _split_in_proj(w_in, gate_lo, gate_lo + 2 * MLSTM_HEADS)
    proj, gates = _norm_matmul(h, norm_g.astype(F32)[None, :], w_main, w_small)
    proj = proj.reshape(b, l, MAIN_WIDTH)
    gates = gates.reshape(b, l, LANES)
    mixed = _mixer_pair(
        [_mlstm(proj, gates, conv_w.astype(F32), conv_b.astype(F32), i_bias, f_bias, mlstm_norm.astype(F32)),
         _s5(proj, a_re, a_im, log_dt, b_re, b_im, c_re, c_im, d_skip, w_glu, b_glu)],
        b, l, "mlstm_s5")
    return _out_proj(mixed.reshape(b * l, -1), w_out.astype(BF16), h)


def kernel(x, even_norm, even_w_in, ret_norm, ssd_conv_w, ssd_conv_b, ssd_dt_bias, ssd_a_log, ssd_d, ssd_norm, even_w_out, odd_norm, odd_w_in, mlstm_conv_w, mlstm_conv_b, mlstm_i_bias, mlstm_f_bias, mlstm_norm, s5_a_re, s5_a_im, s5_log_dt, s5_b_re, s5_b_im, s5_c_re, s5_c_im, s5_d, s5_w_glu, s5_b_glu, odd_w_out, moe_norm, moe_router_g, moe_router_g_b, moe_router_e, moe_router_e_b, moe_w_gate, moe_w_up, moe_w_down, final_norm):
    b, l, d = x.shape
    depth = moe_norm.shape[0]
    h = x.reshape(b * l, d)
    for layer in range(depth):
        i = layer // 2
        if layer % 2 == 0:
            h = _even_layer(h, b, l, even_norm[i], even_w_in[i], ret_norm[i], ssd_conv_w[i], ssd_conv_b[i],
                            ssd_dt_bias[i], ssd_a_log[i], ssd_d[i], ssd_norm[i], even_w_out[i])
        else:
            h = _odd_layer(h, b, l, odd_norm[i], odd_w_in[i], mlstm_conv_w[i], mlstm_conv_b[i], mlstm_i_bias[i],
                           mlstm_f_bias[i], mlstm_norm[i], s5_a_re[i], s5_a_im[i], s5_log_dt[i], s5_b_re[i],
                           s5_b_im[i], s5_c_re[i], s5_c_im[i], s5_d[i], s5_w_glu[i], s5_b_glu[i], odd_w_out[i])
        h = _moe(h, layer, moe_norm[layer], moe_router_g[layer], moe_router_g_b[layer], moe_router_e[layer],
                 moe_router_e_b[layer], moe_w_gate, moe_w_up, moe_w_down, final_norm,
                 final_norm=(layer == depth - 1))
    return h.reshape(b, l, d)
```

```python
import functools
import math

import numpy as np
import jax
import jax.numpy as jnp
from jax import lax
from jax.experimental import pallas as pl
from jax.experimental.pallas import tpu as pltpu

F32 = jnp.float32
BF16 = jnp.bfloat16
HIGHEST = lax.Precision.HIGHEST

D_MODEL = 2048
CHUNK = 128
CONV_W = 4
EPS = 1e-6
LANES = 128
CONV_PAD = 8

RET_HEADS = 8
RET_DIM = 128
RET_WIDTH = RET_HEADS * RET_DIM
ROPE_BASE = 10000.0
SSD_HEAD_DIM = 64
SSD_HEADS = 16
SSD_WIDTH = SSD_HEADS * SSD_HEAD_DIM
SSD_GROUPS = 2
SSD_HPG = SSD_HEADS // SSD_GROUPS
SSD_STATE = 128
SSD_GW = SSD_WIDTH // SSD_GROUPS
MAIN_WIDTH = 6656
MLSTM_HEADS = 4
MLSTM_DIM = 384
MLSTM_WIDTH = MLSTM_HEADS * MLSTM_DIM
S5_GROUP = 16
S5_GROUPS = 32
S5_WIDTH = S5_GROUP * S5_GROUPS
S5_STATE = 64
S5_LANES = S5_GROUPS * S5_STATE
MOE_GROUPS = 4
EXPERTS_PER_GROUP = 8
N_EXPERTS = MOE_GROUPS * EXPERTS_PER_GROUP
MOE_TOP_K = 2
EXPERT_FF = 512
MOE_ROWS = 256
COMBINE_ROWS = 512

VMEM_LIMIT = 56 * 1024 * 1024

IN_PROJ_TILE = (1024, 1664)
OUT_PROJ_TILE = (2048, 512)
ROUTER_ROWS = 512
DISPATCH_ROWS = 1024


def _cparams(sem):
    return pltpu.CompilerParams(dimension_semantics=sem, vmem_limit_bytes=VMEM_LIMIT)


def _dot(a, b):
    return jnp.dot(a, b, preferred_element_type=F32)


def _dot_nt(a, b):
    return lax.dot_general(a, b, (((1,), (1,)), ((), ())), preferred_element_type=F32)


def _dot_tn(a, b):
    return lax.dot_general(a, b, (((0,), (0,)), ((), ())), preferred_element_type=F32)


def _dot_exact(a, b):
    return jnp.dot(a, b, preferred_element_type=F32, precision=HIGHEST)


def _sigmoid(x):
    return 0.5 * jnp.tanh(0.5 * x) + 0.5


def _silu(x):
    return x * _sigmoid(x)


def _softplus(x):
    return jnp.maximum(x, 0.0) + jnp.log1p(jnp.exp(-jnp.abs(x)))


def _tril_f32(n):
    r = lax.broadcasted_iota(jnp.int32, (n, n), 0)
    c = lax.broadcasted_iota(jnp.int32, (n, n), 1)
    return (r >= c).astype(F32)


def _norm_matmul_kernel(x_ref, g_ref, w_ref, w2_ref, o_ref, o2_ref, xn_ref):
    @pl.when(pl.program_id(1) == 0)
    def _():
        x = x_ref[...]
        ms = jnp.mean(x * x, axis=-1, keepdims=True)
        xn = (x * lax.rsqrt(ms + EPS) * g_ref[...]).astype(BF16)
        xn_ref[...] = xn
        o2_ref[...] = _dot(xn, w2_ref[...])

    o_ref[...] = _dot(xn_ref[...], w_ref[...]).astype(o_ref.dtype)


def _norm_matmul(x, g, w, w2):
    t, d = x.shape
    n = w.shape[1]
    tm, tn = min(IN_PROJ_TILE[0], t), IN_PROJ_TILE[1]
    return pl.pallas_call(
        _norm_matmul_kernel,
        grid=(t // tm, n // tn),
        in_specs=[
            pl.BlockSpec((tm, d), lambda i, j: (i, 0)),
            pl.BlockSpec((1, d), lambda i, j: (0, 0)),
            pl.BlockSpec((d, tn), lambda i, j: (0, j)),
            pl.BlockSpec((d, LANES), lambda i, j: (0, 0)),
        ],
        out_specs=[
            pl.BlockSpec((tm, tn), lambda i, j: (i, j)),
            pl.BlockSpec((tm, LANES), lambda i, j: (i, 0)),
        ],
        out_shape=[jax.ShapeDtypeStruct((t, n), BF16), jax.ShapeDtypeStruct((t, LANES), F32)],
        scratch_shapes=[pltpu.VMEM((tm, d), BF16)],
        compiler_params=_cparams(("parallel", "arbitrary")),
        name="norm_in_proj",
    )(x, g, w, w2)


def _out_proj_kernel(a_ref, w_ref, r_ref, o_ref):
    o_ref[...] = r_ref[...] + _dot(a_ref[...], w_ref[...])


def _out_proj(a, w, res):
    t, k = a.shape
    n = w.shape[1]
    tm, tn = min(OUT_PROJ_TILE[0], t), OUT_PROJ_TILE[1]
    return pl.pallas_call(
        _out_proj_kernel,
        grid=(t // tm, n // tn),
        in_specs=[
            pl.BlockSpec((tm, k), lambda i, j: (i, 0)),
            pl.BlockSpec((k, tn), lambda i, j: (0, j)),
            pl.BlockSpec((tm, tn), lambda i, j: (i, j)),
        ],
        out_specs=pl.BlockSpec((tm, tn), lambda i, j: (i, j)),
        out_shape=jax.ShapeDtypeStruct((t, n), F32),
        compiler_params=_cparams(("parallel", "arbitrary")),
        name="out_proj",
    )(a, w, res)


_RET_LOG_GAMMA = [math.log1p(-(2.0 ** (-5.0 - h))) for h in range(RET_HEADS)]


def _rope_tables(seq):
    inv = 1.0 / (ROPE_BASE ** (np.arange(0, RET_DIM, 2, dtype=np.float64) / RET_DIM))
    ang = np.arange(seq, dtype=np.float64)[:, None] * inv[None, :]
    cos = np.cos(ang)
    sin = np.sin(ang)
    cos2 = np.concatenate([cos, cos], axis=1).astype(np.float32)
    sin2 = np.concatenate([-sin, sin], axis=1).astype(np.float32)
    return jnp.asarray(cos2), jnp.asarray(sin2)


def _head_layer_norm(t, g):
    mu = jnp.mean(t, axis=-1, keepdims=True)
    tc = t - mu
    var = jnp.mean(tc * tc, axis=-1, keepdims=True)
    return tc * lax.rsqrt(var + EPS) * g


def _retention_init(st_ref):
    st_ref[...] = jnp.zeros_like(st_ref)


def _retention_body(q_ref, k_ref, v_ref, g_ref, cos_ref, sin_ref, ng_ref, o_ref, st_ref):
    cos = cos_ref[...]
    sin = sin_ref[...]
    row = lax.broadcasted_iota(jnp.int32, (CHUNK, CHUNK), 0)
    col = lax.broadcasted_iota(jnp.int32, (CHUNK, CHUNK), 1)
    rel = (row - col).astype(F32)
    causal = row >= col
    pos = lax.broadcasted_iota(jnp.int32, (CHUNK, 1), 0).astype(F32)
    scale = RET_DIM ** -0.5
    for h in range(RET_HEADS):
        lg = _RET_LOG_GAMMA[h]
        sl = slice(h * RET_DIM, (h + 1) * RET_DIM)
        q = q_ref[:, sl].astype(F32)
        k = k_ref[:, sl].astype(F32)
        q = q * cos + pltpu.roll(q, RET_DIM // 2, 1) * sin
        k = (k * cos + pltpu.roll(k, RET_DIM // 2, 1) * sin) * scale
        qb = q.astype(BF16)
        kb = k.astype(BF16)
        vb = v_ref[:, sl].astype(BF16)
        decay = jnp.where(causal, jnp.exp(lg * jnp.maximum(rel, 0.0)), 0.0)
        scores = _dot_nt(qb, kb) * decay
        state = st_ref[h]
        out = _dot(scores.astype(BF16), vb)
        out = out + _dot(qb, state.astype(BF16)) * jnp.exp(lg * (pos + 1.0))
        kd = (k * jnp.exp(lg * (CHUNK - 1.0 - pos))).astype(BF16)
        st_ref[h] = math.exp(lg * CHUNK) * state + _dot_tn(kd, vb)
        y = _head_layer_norm(out, ng_ref[h:h + 1, :]) * _silu(g_ref[:, sl].astype(F32))
        o_ref[:, sl] = y.astype(o_ref.dtype)


class _MixerPart:
    def __init__(self, init, body, width, in_specs, args, scratch):
        self.init, self.body, self.width = init, body, width
        self.in_specs, self.args, self.scratch = in_specs, args, scratch


def _mixer_pair_kernel(*refs, parts):
    n_in = sum(len(p.args) for p in parts)
    o_ref = refs[n_in]
    ins, scr, lanes = [], [], []
    i0, s0, c0 = 0, n_in + 1, 0
    for p in parts:
        ins.append(refs[i0:i0 + len(p.args)])
        scr.append(refs[s0:s0 + len(p.scratch)])
        lanes.append((c0, c0 + p.width))
        i0, s0, c0 = i0 + len(p.args), s0 + len(p.scratch), c0 + p.width

    @pl.when(pl.program_id(1) == 0)
    def _():
        for p, s in zip(parts, scr):
            p.init(*s)

    for p, a, s, (lo, hi) in zip(parts, ins, scr, lanes):
        p.body(*a, o_ref.at[:, lo:hi], *s)


def _mixer_pair(parts, b, l, name):
    width = sum(p.width for p in parts)
    return pl.pallas_call(
        functools.partial(_mixer_pair_kernel, parts=parts),
        grid=(b, l // CHUNK),
        in_specs=[s for p in parts for s in p.in_specs],
        out_specs=pl.BlockSpec((None, CHUNK, width), lambda bi, c: (bi, c, 0)),
        out_shape=jax.ShapeDtypeStruct((b, l, width), BF16),
        scratch_shapes=[s for p in parts for s in p.scratch],
        compiler_params=_cparams(("parallel", "arbitrary")),
        name=name,
    )(*[a for p in parts for a in p.args])


def _retention(proj, norm_g, cos2, sin2):
    col_spec = lambda j: pl.BlockSpec((None, CHUNK, RET_WIDTH), lambda bi, c: (bi, c, j))
    tab_spec = pl.BlockSpec((CHUNK, RET_DIM), lambda bi, c: (c, 0))
    return _MixerPart(
        _retention_init, _retention_body, RET_WIDTH,
        [col_spec(0), col_spec(1), col_spec(2), col_spec(3), tab_spec, tab_spec,
         pl.BlockSpec((RET_HEADS, RET_DIM), lambda bi, c: (0, 0))],
        (proj, proj, proj, proj, cos2, sin2, norm_g),
        [pltpu.VMEM((RET_HEADS, RET_DIM, RET_DIM), F32)])


def _conv_reset(buf_ref):
    buf_ref[0:CONV_PAD, :] = jnp.zeros((CONV_PAD, buf_ref.shape[1]), F32)


def _conv_silu(x, buf_ref, w_ref, b_ref):
    n = x.shape[0]
    buf_ref[CONV_PAD:CONV_PAD + n, :] = x
    acc = b_ref[...] + w_ref[CONV_W - 1:CONV_W, :] * x
    for j in range(CONV_W - 1):
        off = CONV_PAD - (CONV_W - 1) + j
        acc = acc + w_ref[j:j + 1, :] * buf_ref[off:off + n, :]
    buf_ref[0:CONV_PAD, :] = x[n - CONV_PAD:, :]
    return _silu(acc)


def _ssd_init(st_ref, bufx_ref, bufb_ref):
    st_ref[...] = jnp.zeros_like(st_ref)
    _conv_reset(bufx_ref)
    _conv_reset(bufb_ref)


def _ssd_body(z_ref, xs_ref, bc_ref, dt_ref, cwx_ref, cbx_ref, cwb_ref, cbb_ref, dtb_ref, alog_ref,
              dskip_ref, ng_ref, o_ref, st_ref, bufx_ref, bufb_ref):
    xs = _conv_silu(xs_ref[...].astype(F32), bufx_ref, cwx_ref, cbx_ref)
    bc = _conv_silu(bc_ref[...].astype(F32), bufb_ref, cwb_ref, cbb_ref)
    dt = _softplus(dt_ref[...] + dtb_ref[...])
    da = dt * (-jnp.exp(alog_ref[...]))
    a_cum = _dot_exact(_tril_f32(CHUNK), da)
    a_cum_t = a_cum.T
    a_last = a_cum[CHUNK - 1:CHUNK, :]
    e_cum = jnp.exp(a_cum)
    to_end = jnp.exp(a_last - a_cum)
    e_last = jnp.exp(a_last)
    spread = jnp.where(lax.broadcasted_iota(jnp.int32, (LANES, SSD_WIDTH), 1) // SSD_HEAD_DIM
                       == lax.broadcasted_iota(jnp.int32, (LANES, SSD_WIDTH), 0), 1.0, 0.0)
    xdt_all = xs * _dot_exact(dt, spread)
    xend_all = xdt_all * _dot_exact(to_end, spread)
    e_cum_all = _dot_exact(e_cum, spread)
    row = lax.broadcasted_iota(jnp.int32, (CHUNK, CHUNK), 0)
    col = lax.broadcasted_iota(jnp.int32, (CHUNK, CHUNK), 1)
    causal = row >= col
    gn = SSD_GROUPS * SSD_STATE
    for g in range(SSD_GROUPS):
        bm = bc[:, g * SSD_STATE:(g + 1) * SSD_STATE].astype(BF16)
        cm = bc[:, gn + g * SSD_STATE:gn + (g + 1) * SSD_STATE].astype(BF16)
        cb = _dot_nt(cm, bm)
        ys = []
        for r in range(SSD_HPG):
            hd = g * SSD_HPG + r
            sl = slice(hd * SSD_HEAD_DIM, (hd + 1) * SSD_HEAD_DIM)
            seg = jnp.exp(jnp.where(causal, a_cum[:, hd:hd + 1] - a_cum_t[hd:hd + 1, :], -jnp.inf))
            state = st_ref[hd]
            y = _dot((cb * seg).astype(BF16), xdt_all[:, sl].astype(BF16))
            y = y + _dot(cm, state.astype(BF16)) * e_cum_all[:, sl]
            st_ref[hd] = e_last[:, hd:hd + 1] * state + _dot_tn(bm, xend_all[:, sl].astype(BF16))
            ys.append(y + dskip_ref[:, sl] * xs[:, sl])
        gsl = slice(g * SSD_GW, (g + 1) * SSD_GW)
        y = jnp.concatenate(ys, axis=1) * _silu(z_ref[:, gsl].astype(F32))
        y = y * lax.rsqrt(jnp.mean(y * y, axis=-1, keepdims=True) + EPS)
        o_ref[:, gsl] = (y * ng_ref[:, gsl]).astype(o_ref.dtype)


def _ssd(proj, gates, conv_w, conv_b, dt_bias, a_log, d_skip, norm_g):
    bcw = 2 * SSD_GROUPS * SSD_STATE
    pad = lambda v: jnp.pad(v.astype(F32), (0, LANES - v.shape[0]))[None, :]
    full = lambda shape: pl.BlockSpec(shape, lambda bi, c: (0,) * len(shape))
    z_off = 4 * RET_WIDTH // SSD_WIDTH
    return _MixerPart(
        _ssd_init, _ssd_body, SSD_WIDTH,
        [
            pl.BlockSpec((None, CHUNK, SSD_WIDTH), lambda bi, c: (bi, c, z_off)),
            pl.BlockSpec((None, CHUNK, SSD_WIDTH), lambda bi, c: (bi, c, z_off + 1)),
            pl.BlockSpec((None, CHUNK, bcw), lambda bi, c: (bi, c, (z_off + 2) * SSD_WIDTH // bcw)),
            pl.BlockSpec((None, CHUNK, LANES), lambda bi, c: (bi, c, 0)),
            full((CONV_W, SSD_WIDTH)), full((1, SSD_WIDTH)), full((CONV_W, bcw)), full((1, bcw)),
            full((1, LANES)), full((1, LANES)), full((1, SSD_WIDTH)), full((1, SSD_WIDTH)),
        ],
        (proj, proj, proj, gates,
         conv_w[:, :SSD_WIDTH], conv_b[None, :SSD_WIDTH], conv_w[:, SSD_WIDTH:], conv_b[None, SSD_WIDTH:],
         pad(dt_bias), pad(a_log), jnp.repeat(d_skip.astype(F32), SSD_HEAD_DIM)[None, :], norm_g[None, :]),
        [
            pltpu.VMEM((SSD_HEADS, SSD_STATE, SSD_HEAD_DIM), F32),
            pltpu.VMEM((CONV_PAD + CHUNK, SSD_WIDTH), F32),
            pltpu.VMEM((CONV_PAD + CHUNK, bcw), F32),
        ])


def _mlstm_init(c_ref, n_ref, m_ref, bufq_ref, bufk_ref):
    c_ref[...] = jnp.zeros_like(c_ref)
    n_ref[...] = jnp.zeros_like(n_ref)
    m_ref[...] = jnp.full(m_ref.shape, -jnp.inf, F32)
    _conv_reset(bufq_ref)
    _conv_reset(bufk_ref)


def _mlstm_body(q_ref, k_ref, v_ref, o_ref, gt_ref, cwq_ref, cbq_ref, cwk_ref, cbk_ref, gb_ref, ng_ref,
                out_ref, c_ref, n_ref, m_ref, bufq_ref, bufk_ref):
    nh = MLSTM_HEADS
    pre = gt_ref[...] + gb_ref[...]
    bcum_all = _dot_exact(_tril_f32(CHUNK), -_softplus(-pre))
    pre_t = pre.T
    bcum_all_t = bcum_all.T
    row = lax.broadcasted_iota(jnp.int32, (CHUNK, CHUNK), 0)
    col = lax.broadcasted_iota(jnp.int32, (CHUNK, CHUNK), 1)
    causal = row >= col

    for hd in range(nh):
        sl = slice(hd * MLSTM_DIM, (hd + 1) * MLSTM_DIM)
        q = _conv_silu(q_ref[:, sl].astype(F32), bufq_ref.at[:, sl], cwq_ref.at[:, sl], cbq_ref.at[:, sl])
        q = q * (MLSTM_DIM ** -0.5)
        k = _conv_silu(k_ref[:, sl].astype(F32), bufk_ref.at[:, sl], cwk_ref.at[:, sl], cbk_ref.at[:, sl])
        qb = q.astype(BF16)
        kb = k.astype(BF16)
        vb = v_ref[:, sl].astype(BF16)
        ig = pre[:, hd:hd + 1]
        ig_t = pre_t[hd:hd + 1, :]
        bcum = bcum_all[:, nh + hd:nh + hd + 1]
        bcum_t = bcum_all_t[nh + hd:nh + hd + 1, :]

        log_d = jnp.where(causal, bcum - bcum_t + ig_t, -jnp.inf)
        m_prev = m_ref[hd]
        m_inter = bcum + m_prev
        m_t = jnp.maximum(jnp.max(log_d, axis=1, keepdims=True), m_inter)
        scores = _dot_nt(qb, kb) * jnp.exp(log_d - m_t)
        inter_scale = jnp.exp(m_inter - m_t)
        c_st = c_ref[hd]
        n_st = n_ref[hd]
        num = _dot(scores.astype(BF16), vb) + inter_scale * _dot(qb, c_st.astype(BF16))
        den = jnp.sum(scores, axis=1, keepdims=True) + inter_scale * jnp.sum(q * n_st, axis=1, keepdims=True)
        hval = num / jnp.maximum(jnp.abs(den), jnp.exp(-m_t))

        b_last = bcum[CHUNK - 1:CHUNK, :]
        w_log = b_last - bcum + ig
        m_new = jnp.maximum(b_last + m_prev, jnp.max(w_log, axis=0, keepdims=True))
        kw = k * jnp.exp(w_log - m_new)
        prev_scale = jnp.exp(b_last + m_prev - m_new)
        c_ref[hd] = prev_scale * c_st + _dot_tn(kw.astype(BF16), vb)
        n_ref[hd] = prev_scale * n_st + jnp.sum(kw, axis=0, keepdims=True)
        m_ref[hd] = m_new

        y = _head_layer_norm(hval, ng_ref[:, sl]) * _sigmoid(o_ref[:, sl].astype(F32))
        out_ref[:, sl] = y.astype(out_ref.dtype)


def _mlstm(proj, gates, conv_w, conv_b, i_bias, f_bias, norm_g):
    nh = MLSTM_HEADS
    w = MLSTM_WIDTH
    col_spec = lambda j: pl.BlockSpec((None, CHUNK, w), lambda bi, c: (bi, c, j))
    cw_spec = lambda j: pl.BlockSpec((CONV_W, w), lambda bi, c: (0, j))
    cb_spec = lambda j: pl.BlockSpec((1, w), lambda bi, c: (0, j))
    gate_bias = jnp.pad(jnp.concatenate([i_bias, f_bias]).astype(F32), (0, LANES - 2 * nh))[None, :]
    return _MixerPart(
        _mlstm_init, _mlstm_body, w,
        [
            col_spec(0), col_spec(1), col_spec(2), col_spec(3),
            pl.BlockSpec((None, CHUNK, LANES), lambda bi, c: (bi, c, 0)),
            cw_spec(0), cb_spec(0), cw_spec(1), cb_spec(1),
            pl.BlockSpec((1, LANES), lambda bi, c: (0, 0)),
            pl.BlockSpec((1, w), lambda bi, c: (0, 0)),
        ],
        (proj, proj, proj, proj, gates, conv_w, conv_b[None, :], conv_w, conv_b[None, :], gate_bias,
         norm_g.reshape(1, w)),
        [
            pltpu.VMEM((nh, MLSTM_DIM, MLSTM_DIM), F32),
            pltpu.VMEM((nh, 1, MLSTM_DIM), F32),
            pltpu.VMEM((nh, 1, 1), F32),
            pltpu.VMEM((CONV_PAD + CHUNK, w), F32),
            pltpu.VMEM((CONV_PAD + CHUNK, w), F32),
        ])


S5_SEG = CHUNK // CONV_PAD
S5_SLAB = 128
S5_SLAB_LANES = S5_SLAB // S5_GROUP * S5_STATE
_S5_SEG_LOG = int(math.log2(CONV_PAD))


def _cmul(a_re, a_im, b_re, b_im):
    return a_re * b_re - a_im * b_im, a_re * b_im + a_im * b_re


def _s5_init(sre_ref, sim_ref, st_re_ref, st_im_ref, ucol_ref):
    st_re_ref[...] = jnp.zeros_like(st_re_ref)
    st_im_ref[...] = jnp.zeros_like(st_im_ref)


def _s5_body(u_ref, bre_ref, bim_ref, pt_re_ref, pt_im_ref, sg_re_ref, sg_im_ref, cre_ref, cim_ref,
             dskip_ref, wglu_ref, bglu_ref, o_ref, sre_ref, sim_ref, st_re_ref, st_im_ref, ucol_ref):
    for q in range(S5_WIDTH // LANES):
        ucol_ref[q] = u_ref[:, q * LANES:(q + 1) * LANES].astype(F32)
    u = jnp.concatenate(
        [jnp.concatenate([ucol_ref[q, pl.ds(v, CONV_PAD, stride=S5_SEG), :] for v in range(S5_SEG)], axis=0)
         for q in range(S5_WIDTH // LANES)], axis=1)
    ub = u.astype(BF16)
    sub = lax.broadcasted_iota(jnp.int32, (CONV_PAD, LANES), 0)
    grp = lambda x, v: x[v * CONV_PAD:(v + 1) * CONV_PAD, :]
    for jj in range(S5_WIDTH // S5_SLAB):
        ch = slice(jj * S5_SLAB, (jj + 1) * S5_SLAB)
        xr = _dot(ub[:, ch], bre_ref[ch, jj * S5_SLAB_LANES:(jj + 1) * S5_SLAB_LANES])
        xi = _dot(ub[:, ch], bim_ref[ch, jj * S5_SLAB_LANES:(jj + 1) * S5_SLAB_LANES])
        for q in range(S5_SLAB_LANES // LANES):
            sl = slice(jj * S5_SLAB_LANES + q * LANES, jj * S5_SLAB_LANES + (q + 1) * LANES)
            lre = xr[:, q * LANES:(q + 1) * LANES]
            lim = xi[:, q * LANES:(q + 1) * LANES]
            a_re = pt_re_ref[0:CONV_PAD, sl]
            a_im = pt_im_ref[0:CONV_PAD, sl]
            s_re = [grp(lre, 0)]
            s_im = [grp(lim, 0)]
            for v in range(1, S5_SEG):
                d_re, d_im = _cmul(a_re, a_im, s_re[-1], s_im[-1])
                s_re.append(grp(lre, v) + d_re)
                s_im.append(grp(lim, v) + d_im)
            c_re = st_re_ref[:, sl]
            c_im = st_im_ref[:, sl]
            i_re, i_im = _cmul(sg_re_ref[0:1, sl], sg_im_ref[0:1, sl], c_re, c_im)
            e_re = s_re[-1] + jnp.where(sub == 0, i_re, 0.0)
            e_im = s_im[-1] + jnp.where(sub == 0, i_im, 0.0)
            for k in range(_S5_SEG_LOG):
                d = 1 << k
                h_re = jnp.where(sub >= d, pltpu.roll(e_re, d, 0), 0.0)
                h_im = jnp.where(sub >= d, pltpu.roll(e_im, d, 0), 0.0)
                d_re, d_im = _cmul(sg_re_ref[k:k + 1, sl], sg_im_ref[k:k + 1, sl], h_re, h_im)
                e_re = e_re + d_re
                e_im = e_im + d_im
            st_re_ref[:, sl] = e_re[CONV_PAD - 1:CONV_PAD, :]
            st_im_ref[:, sl] = e_im[CONV_PAD - 1:CONV_PAD, :]
            p_re = jnp.where(sub >= 1, pltpu.roll(e_re, 1, 0), c_re)
            p_im = jnp.where(sub >= 1, pltpu.roll(e_im, 1, 0), c_im)
            for v in range(S5_SEG):
                rows = slice(v * CONV_PAD, (v + 1) * CONV_PAD)
                d_re, d_im = _cmul(pt_re_ref[rows, sl], pt_im_ref[rows, sl], p_re, p_im)
                s_re[v] = s_re[v] + d_re
                s_im[v] = s_im[v] + d_im
            sre_ref[:, sl] = jnp.concatenate(s_re, axis=0).astype(BF16)
            sim_ref[:, sl] = jnp.concatenate(s_im, axis=0).astype(BF16)

    ys = []
    for jj in range(S5_WIDTH // S5_SLAB):
        ch = slice(jj * S5_SLAB, (jj + 1) * S5_SLAB)
        ln = slice(jj * S5_SLAB_LANES, (jj + 1) * S5_SLAB_LANES)
        ys.append(_dot(sre_ref[:, ln], cre_ref[ln, ch]) - _dot(sim_ref[:, ln], cim_ref[ln, ch]))
    y = jnp.concatenate(ys, axis=1) + dskip_ref[...] * u
    y = 0.5 * y * (1.0 + jnp.tanh(math.sqrt(2.0 / math.pi) * (y + 0.044715 * (y * y * y))))
    vg = _dot(y.astype(BF16), wglu_ref[...]) + bglu_ref[...]
    res = (vg[:, :S5_WIDTH] * _sigmoid(vg[:, S5_WIDTH:])).astype(BF16)
    t_idx = lax.broadcasted_iota(jnp.int32, (CHUNK, CHUNK), 0)
    p_idx = lax.broadcasted_iota(jnp.int32, (CHUNK, CHUNK), 1)
    unperm = jnp.where(p_idx == CONV_PAD * (t_idx % S5_SEG) + t_idx // S5_SEG, 1.0, 0.0).astype(BF16)
    o_ref[...] = _dot(unperm, res).astype(o_ref.dtype)


def _s5_params(a_re, a_im, log_dt, b_re, b_im, c_re, c_im):
    a_re = a_re.astype(F32)
    a_im = a_im.astype(F32)
    dt = jnp.exp(log_dt.astype(F32))[:, None]
    mag = jnp.exp(a_re * dt)
    abar_re = mag * jnp.cos(a_im * dt)
    abar_im = mag * jnp.sin(a_im * dt)
    den = a_re * a_re + a_im * a_im
    nr = abar_re - 1.0
    coef_re = (nr * a_re + abar_im * a_im) / den
    coef_im = (abar_im * a_re - nr * a_im) / den
    b_re = b_re.astype(F32)
    b_im = b_im.astype(F32)
    bbar_re = coef_re[..., None] * b_re - coef_im[..., None] * b_im
    bbar_im = coef_re[..., None] * b_im + coef_im[..., None] * b_re
    eye = jnp.eye(S5_GROUPS, dtype=F32)
    in_map = lambda m: jnp.einsum('gnc,gh->gchn', m, eye).reshape(S5_WIDTH, S5_LANES).astype(BF16)
    out_map = lambda m: jnp.einsum('gcn,gh->gnhc', m.astype(F32), eye).reshape(S5_LANES, S5_WIDTH).astype(BF16)

    def powers(exps):
        e = exps[:, None, None]
        m = jnp.exp(e * (a_re * dt)[None])
        ang = e * (a_im * dt)[None]
        return (m * jnp.cos(ang)).reshape(-1, S5_LANES), (m * jnp.sin(ang)).reshape(-1, S5_LANES)

    pt_re, pt_im = powers(jnp.repeat(jnp.arange(1, S5_SEG + 1, dtype=F32), CONV_PAD))
    sg_re, sg_im = powers(jnp.asarray([float(S5_SEG << k) for k in range(_S5_SEG_LOG)], F32))
    pad8 = lambda m: jnp.pad(m, ((0, CONV_PAD - m.shape[0]), (0, 0)))
    return (in_map(bbar_re), in_map(bbar_im), pt_re, pt_im, pad8(sg_re), pad8(sg_im),
            out_map(c_re), out_map(c_im))


def _s5(proj, a_re, a_im, log_dt, b_re, b_im, c_re, c_im, d_skip, w_glu, b_glu):
    tabs = _s5_params(a_re, a_im, log_dt, b_re, b_im, c_re, c_im)
    full = lambda shape: pl.BlockSpec(shape, lambda bi, c: (0,) * len(shape))
    u_off = 4 * MLSTM_WIDTH // S5_WIDTH
    return _MixerPart(
        _s5_init, _s5_body, S5_WIDTH,
        [
            pl.BlockSpec((None, CHUNK, S5_WIDTH), lambda bi, c: (bi, c, u_off)),
            full((S5_WIDTH, S5_LANES)), full((S5_WIDTH, S5_LANES)),
            full((CHUNK, S5_LANES)), full((CHUNK, S5_LANES)),
            full((CONV_PAD, S5_LANES)), full((CONV_PAD, S5_LANES)),
            full((S5_LANES, S5_WIDTH)), full((S5_LANES, S5_WIDTH)),
            full((1, S5_WIDTH)), full((S5_WIDTH, 2 * S5_WIDTH)), full((1, 2 * S5_WIDTH)),
        ],
        (proj, *tabs, d_skip.astype(F32).reshape(1, S5_WIDTH), w_glu.astype(BF16), b_glu.astype(F32)[None, :]),
        [
            pltpu.VMEM((CHUNK, S5_LANES), BF16), pltpu.VMEM((CHUNK, S5_LANES), BF16),
            pltpu.VMEM((1, S5_LANES), F32), pltpu.VMEM((1, S5_LANES), F32),
            pltpu.VMEM((S5_WIDTH // LANES, CHUNK, LANES), F32),
        ])


def _router_kernel(x_ref, g_ref, whi_ref, wlo_ref, b_ref, ids_ref, wts_ref, cnt_ref, carry_ref):
    @pl.when(pl.program_id(0) == 0)
    def _():
        carry_ref[...] = jnp.zeros_like(carry_ref)

    x = x_ref[...]
    xn = x * lax.rsqrt(jnp.mean(x * x, axis=-1, keepdims=True) + EPS) * g_ref[...]
    xh = xn.astype(BF16)
    xl = (xn - xh.astype(F32)).astype(BF16)
    logits = _dot(xh, whi_ref[...]) + (_dot(xl, whi_ref[...]) + _dot(xh, wlo_ref[...])) + b_ref[...]
    lane = lax.broadcasted_iota(jnp.int32, logits.shape, 1)
    big = jnp.int32(LANES)
    neg = -jnp.inf

    def top(vals):
        m = jnp.max(vals, axis=1, keepdims=True)
        idx = jnp.min(jnp.where(vals == m, lane, big), axis=1, keepdims=True)
        return m, idx

    gl = jnp.where(lane < MOE_GROUPS, logits, neg)
    g_max, g_idx = top(gl)
    g_w = 1.0 / jnp.sum(jnp.exp(gl - g_max), axis=1, keepdims=True)
    lo = MOE_GROUPS + EXPERTS_PER_GROUP * g_idx
    el = jnp.where((lane >= lo) & (lane < lo + EXPERTS_PER_GROUP), logits, neg)
    m1, j1 = top(el)
    m2, j2 = top(jnp.where(lane == j1, neg, el))
    e2 = jnp.exp(m2 - m1)
    w1 = g_w / (1.0 + e2)
    w2 = g_w * e2 / (1.0 + e2)
    e1 = j1 - MOE_GROUPS
    e2 = j2 - MOE_GROUPS

    tm = x.shape[0]
    hit = jnp.where(lane == e1, 1.0, jnp.where(lane == e2, 1.0, 0.0))
    row = lax.broadcasted_iota(jnp.int32, (tm, tm), 0)
    col = lax.broadcasted_iota(jnp.int32, (tm, tm), 1)
    before = jnp.where(row > col, 1.0, 0.0).astype(BF16)
    prefix = _dot(before, hit.astype(BF16)) + carry_ref[...]
    r1 = jnp.sum(jnp.where(lane == e1, prefix, 0.0), axis=1, keepdims=True)
    r2 = jnp.sum(jnp.where(lane == e2, prefix, 0.0), axis=1, keepdims=True)
    carry_ref[...] = carry_ref[...] + jnp.sum(hit, axis=0, keepdims=True)
    cnt_ref[...] = carry_ref[...]

    wts_ref[...] = jnp.where(lane == 0, w1, jnp.where(lane == 1, w2, 0.0))
    ids = jnp.where(lane == 0, e1.astype(F32), jnp.where(lane == 1, e2.astype(F32),
                                                         jnp.where(lane == 2, r1, jnp.where(lane == 3, r2, 0.0))))
    for c in range(tm // LANES):
        ids_ref[:, c * LANES:(c + 1) * LANES] = ids[c * LANES:(c + 1) * LANES, :].T[0:CONV_PAD, :].astype(jnp.int32)


def _router(x, g, w, bias):
    t, d = x.shape
    tm = min(ROUTER_ROWS, t)
    w_hi = w.astype(BF16)
    w_lo = (w - w_hi.astype(F32)).astype(BF16)
    return pl.pallas_call(
        _router_kernel,
        grid=(t // tm,),
        in_specs=[
            pl.BlockSpec((tm, d), lambda i: (i, 0)),
            pl.BlockSpec((1, d), lambda i: (0, 0)),
            pl.BlockSpec((d, LANES), lambda i: (0, 0)),
            pl.BlockSpec((d, LANES), lambda i: (0, 0)),
            pl.BlockSpec((1, LANES), lambda i: (0, 0)),
        ],
        out_specs=[pl.BlockSpec((CONV_PAD, tm), lambda i: (0, i)), pl.BlockSpec((tm, LANES), lambda i: (i, 0)),
                   pl.BlockSpec((1, LANES), lambda i: (0, 0))],
        out_shape=[jax.ShapeDtypeStruct((CONV_PAD, t), jnp.int32), jax.ShapeDtypeStruct((t, LANES), F32),
                   jax.ShapeDtypeStruct((1, LANES), F32)],
        scratch_shapes=[pltpu.VMEM((1, LANES), F32)],
        compiler_params=_cparams(("arbitrary",)),
        name="moe_router",
    )(x, g, w_hi, w_lo, bias)


def _row_gather_copy(src_hbm, idx_ref, base, buf_ref, slot, sem_ref, r):
    return pltpu.make_async_copy(src_hbm.at[pl.ds(idx_ref[base + r], 1)], buf_ref.at[slot, pl.ds(r, 1)],
                                 sem_ref.at[slot])


def _start_row_gather(src_hbm, idx_ref, base, buf_ref, slot, sem_ref, rows):
    def body(r, carry):
        _row_gather_copy(src_hbm, idx_ref, base, buf_ref, slot, sem_ref, r).start()
        return carry

    lax.fori_loop(0, rows, body, 0)


def _start_row_gather_inline(src_hbm, idx_ref, base, buf_ref, slot, sem_ref, rows, alternate):
    for r in range(rows):
        _row_gather_copy(src_hbm, idx_ref, base, buf_ref, slot, sem_ref, r).start(priority=r % 2 if alternate else 0)


def _wait_row_gather(src_hbm, buf_ref, slot, sem_ref, rows):
    pltpu.make_async_copy(src_hbm.at[pl.ds(0, rows)], buf_ref.at[slot], sem_ref.at[slot]).wait()


def _dispatch_kernel(d0_ref, d1_ref, padrow_ref, padcnt_ref, x_ref, xs_hbm, sem_ref, zsem_ref, zero_ref):
    i = pl.program_id(0)
    rows = x_ref.shape[0]
    base = i * rows

    def zero_copy(row):
        return pltpu.make_async_copy(zero_ref.at[pl.ds(0, 1)], xs_hbm.at[pl.ds(row, 1)], zsem_ref.at[0])

    def for_pad_rows(fn):
        def per_expert(e, carry):
            first = padrow_ref[e]

            def per_row(j, c):
                fn(first + j)
                return c

            lax.fori_loop(0, padcnt_ref[e], per_row, 0)
            return carry

        lax.fori_loop(0, N_EXPERTS, per_expert, 0)

    def tail_copy(j):
        row = pl.multiple_of(padrow_ref[N_EXPERTS] + j * CONV_PAD, CONV_PAD)
        return pltpu.make_async_copy(zero_ref, xs_hbm.at[pl.ds(row, CONV_PAD)], zsem_ref.at[1])

    def for_tail(fn):
        def per_group(j, carry):
            fn(j)
            return carry

        lax.fori_loop(0, padcnt_ref[N_EXPERTS] // CONV_PAD, per_group, 0)

    @pl.when(i == 0)
    def _():
        zero_ref[...] = jnp.zeros_like(zero_ref)
        for_pad_rows(lambda row: zero_copy(row).start())
        for_tail(lambda j: tail_copy(j).start())

    def body(r, carry):
        src = x_ref.at[pl.ds(r, 1)]
        pltpu.make_async_copy(src, xs_hbm.at[pl.ds(d0_ref[base + r], 1)], sem_ref.at[0]).start()
        pltpu.make_async_copy(src, xs_hbm.at[pl.ds(d1_ref[base + r], 1)], sem_ref.at[1]).start(priority=1)
        return carry

    lax.fori_loop(0, rows, body, 0, unroll=8)
    for k in range(MOE_TOP_K):
        pltpu.make_async_copy(x_ref, xs_hbm.at[pl.ds(0, rows)], sem_ref.at[k]).wait()

    @pl.when(i == pl.num_programs(0) - 1)
    def _():
        for_pad_rows(lambda row: zero_copy(row).wait())
        for_tail(lambda j: tail_copy(j).wait())


def _dispatch(x, dest0, dest1, pad_row, pad_cnt, n_rows):
    t, w = x.shape
    tm = min(DISPATCH_ROWS, t)
    grid_spec = pltpu.PrefetchScalarGridSpec(
        num_scalar_prefetch=4,
        grid=(t // tm,),
        in_specs=[pl.BlockSpec((tm, w), lambda i, *_: (i, 0))],
        out_specs=pl.BlockSpec(memory_space=pl.ANY),
        scratch_shapes=[pltpu.SemaphoreType.DMA((MOE_TOP_K,)), pltpu.SemaphoreType.DMA((2,)),
                        pltpu.VMEM((CONV_PAD, w), x.dtype)],
    )
    return pl.pallas_call(
        _dispatch_kernel,
        grid_spec=grid_spec,
        out_shape=jax.ShapeDtypeStruct((n_rows, w), x.dtype),
        compiler_params=_cparams(("arbitrary",)),
        name="moe_dispatch",
    )(dest0, dest1, pad_row, pad_cnt, x)


def _expert_kernel(be_ref, first_ref, par_ref, nxt_ref, nused_ref, x_ref, g_ref, wg_hbm, wu_hbm, wd_hbm,
                   o_ref, wgf_ref, wuf_ref, wdf_ref, wsem_ref, wgb_ref, wub_ref, wdb_ref, *, layer):
    i = pl.program_id(0)
    n_used = nused_ref[0]

    def weight_copies(e, p):
        return (pltpu.make_async_copy(wg_hbm.at[layer, e], wgf_ref.at[p], wsem_ref.at[0, p]),
                pltpu.make_async_copy(wu_hbm.at[layer, e], wuf_ref.at[p], wsem_ref.at[1, p]),
                pltpu.make_async_copy(wd_hbm.at[layer, e], wdf_ref.at[p], wsem_ref.at[2, p]))

    @pl.when(i == 0)
    def _():
        for c in weight_copies(be_ref[0], 0):
            c.start(priority=1)

    @pl.when((i < n_used) & (first_ref[i] == 1))
    def _():
        p = par_ref[i]
        for c in weight_copies(be_ref[i], p):
            c.wait()
        nxt = nxt_ref[i]

        @pl.when(nxt >= 0)
        def _():
            for c in weight_copies(nxt, 1 - p):
                c.start(priority=1)

        wgb_ref[...] = wgf_ref[p].astype(BF16)
        wub_ref[...] = wuf_ref[p].astype(BF16)
        wdb_ref[...] = wdf_ref[p].astype(BF16)

    @pl.when(i < n_used)
    def _():
        x = x_ref[...]
        xn = (x * lax.rsqrt(jnp.mean(x * x, axis=-1, keepdims=True) + EPS) * g_ref[...]).astype(BF16)
        act = (_silu(_dot(xn, wgb_ref[...])) * _dot(xn, wub_ref[...])).astype(BF16)
        o_ref[...] = _dot(act, wdb_ref[...])

    @pl.when(i >= n_used)
    def _():
        o_ref[...] = jnp.zeros_like(o_ref)


def _expert_ffn(x_sorted, g, tables, n_used, w_gate, w_up, w_down, layer):
    n_rows, d = x_sorted.shape
    n_blocks = n_rows // MOE_ROWS
    ff = w_gate.shape[-1]
    any_spec = pl.BlockSpec(memory_space=pl.ANY)
    grid_spec = pltpu.PrefetchScalarGridSpec(
        num_scalar_prefetch=5,
        grid=(n_blocks,),
        in_specs=[pl.BlockSpec((MOE_ROWS, d), lambda i, be, fi, pa, nx, nu: (jnp.minimum(i, nu[0] - 1), 0)),
                  pl.BlockSpec((1, d), lambda i, *_: (0, 0)), any_spec, any_spec, any_spec],
        out_specs=pl.BlockSpec((MOE_ROWS, d), lambda i, *_: (i, 0)),
        scratch_shapes=[
            pltpu.VMEM((2, d, ff), F32), pltpu.VMEM((2, d, ff), F32), pltpu.VMEM((2, ff, d), F32),
            pltpu.SemaphoreType.DMA((3, 2)),
            pltpu.VMEM((d, ff), BF16), pltpu.VMEM((d, ff), BF16), pltpu.VMEM((ff, d), BF16),
        ],
    )
    return pl.pallas_call(
        functools.partial(_expert_kernel, layer=layer),
        grid_spec=grid_spec,
        out_shape=jax.ShapeDtypeStruct((n_rows, d), F32),
        compiler_params=_cparams(("arbitrary",)),
        name="moe_experts",
    )(*tables, n_used, x_sorted, g, w_gate, w_up, w_down)


def _combine_kernel(d0_ref, d1_ref, y_hbm, h_ref, w_ref, fg_ref, o_ref, buf0_ref, buf1_ref, sem0_ref, sem1_ref, *,
                    final_norm):
    i = pl.program_id(0)
    n = pl.num_programs(0)
    slot = lax.rem(i, 2)
    next_base = lax.rem(i + 1, n) * COMBINE_ROWS

    @pl.when(i == 0)
    def _():
        _start_row_gather(y_hbm, d0_ref, 0, buf0_ref, 0, sem0_ref, COMBINE_ROWS)
        _start_row_gather(y_hbm, d1_ref, 0, buf1_ref, 0, sem1_ref, COMBINE_ROWS)

    _wait_row_gather(y_hbm, buf0_ref, slot, sem0_ref, COMBINE_ROWS)
    _wait_row_gather(y_hbm, buf1_ref, slot, sem1_ref, COMBINE_ROWS)
    _start_row_gather_inline(y_hbm, d0_ref, next_base, buf0_ref, 1 - slot, sem0_ref, COMBINE_ROWS, True)
    _start_row_gather_inline(y_hbm, d1_ref, next_base, buf1_ref, 1 - slot, sem1_ref, COMBINE_ROWS, True)
    w = w_ref[...]
    y = h_ref[...] + (w[:, 0:1] * buf0_ref[slot] + w[:, 1:2] * buf1_ref[slot])
    if final_norm:
        y = y * lax.rsqrt(jnp.mean(y * y, axis=-1, keepdims=True) + EPS) * fg_ref[...]
    o_ref[...] = y

    @pl.when(i == n - 1)
    def _():
        _wait_row_gather(y_hbm, buf0_ref, 1 - slot, sem0_ref, COMBINE_ROWS)
        _wait_row_gather(y_hbm, buf1_ref, 1 - slot, sem1_ref, COMBINE_ROWS)


def _combine(y_rows, dest0, dest1, h, wts, final_g, final_norm):
    t, d = h.shape
    grid_spec = pltpu.PrefetchScalarGridSpec(
        num_scalar_prefetch=2,
        grid=(t // COMBINE_ROWS,),
        in_specs=[
            pl.BlockSpec(memory_space=pl.ANY),
            pl.BlockSpec((COMBINE_ROWS, d), lambda i, d0, d1: (i, 0)),
            pl.BlockSpec((COMBINE_ROWS, LANES), lambda i, d0, d1: (i, 0)),
            pl.BlockSpec((1, d), lambda i, d0, d1: (0, 0)),
        ],
        out_specs=pl.BlockSpec((COMBINE_ROWS, d), lambda i, d0, d1: (i, 0)),
        scratch_shapes=[
            pltpu.VMEM((2, COMBINE_ROWS, d), F32), pltpu.VMEM((2, COMBINE_ROWS, d), F32),
            pltpu.SemaphoreType.DMA((2,)), pltpu.SemaphoreType.DMA((2,)),
        ],
    )
    return pl.pallas_call(
        functools.partial(_combine_kernel, final_norm=final_norm),
        grid_spec=grid_spec,
        out_shape=jax.ShapeDtypeStruct((t, d), F32),
        compiler_params=_cparams(("arbitrary",)),
        name="moe_combine",
    )(dest0, dest1, y_rows, h, wts, final_g)


def _moe(h, layer, norm_g, router_g, router_g_b, router_e, router_e_b, w_gate, w_up, w_down, final_g, final_norm):
    t, d = h.shape
    g = norm_g.astype(F32)[None, :]
    pad_cols = LANES - MOE_GROUPS - N_EXPERTS
    w_r = jnp.pad(jnp.concatenate([router_g, router_e], axis=1).astype(F32), ((0, 0), (0, pad_cols)))
    b_r = jnp.pad(jnp.concatenate([router_g_b, router_e_b]).astype(F32), (0, pad_cols))[None, :]
    ids, wts, cnt = _router(h, g, w_r, b_r)

    n_slots = t * MOE_TOP_K
    n_blocks = (n_slots + N_EXPERTS * (MOE_ROWS - 1) + MOE_ROWS - 1) // MOE_ROWS
    n_rows = n_blocks * MOE_ROWS
    counts = cnt[0, :N_EXPERTS].astype(jnp.int32)
    blocks_of = (counts + MOE_ROWS - 1) // MOE_ROWS
    blk_end = jnp.cumsum(blocks_of)
    blk_start = blk_end - blocks_of
    n_used = blk_end[-1:]
    blk = jnp.arange(n_blocks, dtype=jnp.int32)
    block_expert = jnp.minimum(jnp.sum((blk_end[None, :] <= blk[:, None]).astype(jnp.int32), axis=1), N_EXPERTS - 1)
    first = (blk == blk_start[block_expert]).astype(jnp.int32)
    parity = (jnp.cumsum(first) - 1) % 2
    nxt_blk = blk_end[block_expert]
    nxt_expert = jnp.where(nxt_blk < n_used[0], block_expert[jnp.minimum(nxt_blk, n_blocks - 1)], -1)
    tables = (block_expert, first, parity.astype(jnp.int32), nxt_expert.astype(jnp.int32))

    experts = jnp.arange(N_EXPERTS, dtype=jnp.int32)[:, None, None]
    first_row = jnp.sum(jnp.where(ids[None, 0:MOE_TOP_K] == experts, (blk_start * MOE_ROWS)[:, None, None], 0), axis=0)
    dest = first_row + ids[MOE_TOP_K:2 * MOE_TOP_K]

    pad_row = jnp.concatenate([blk_start * MOE_ROWS + counts, n_used * MOE_ROWS])
    pad_cnt = jnp.concatenate([blocks_of * MOE_ROWS - counts, n_rows - n_used * MOE_ROWS])
    x_sorted = _dispatch(h, dest[0], dest[1], pad_row, pad_cnt, n_rows)
    y_rows = _expert_ffn(x_sorted, g, tables, n_used, w_gate, w_up, w_down, layer)
    return _combine(y_rows, dest[0], dest[1], h, wts, final_g.astype(F32)[None, :], final_norm)


def _split_in_proj(w_in, gate_lo, gate_hi):
    main = jnp.concatenate([w_in[:, :gate_lo], w_in[:, gate_hi:]], axis=1).astype(BF16)
    small = jnp.pad(w_in[:, gate_lo:gate_hi], ((0, 0), (0, LANES - (gate_hi - gate_lo)))).astype(BF16)
    return main, small


def _even_layer(h, b, l, norm_g, w_in, ret_norm, conv_w, conv_b, dt_bias, a_log, d_skip, ssd_norm, w_out):
    w_main, w_small = _split_in_proj(w_in, MAIN_WIDTH, MAIN_WIDTH + SSD_HEADS)
    proj, gates = _norm_matmul(h, norm_g.astype(F32)[None, :], w_main, w_small)
    proj = proj.reshape(b, l, MAIN_WIDTH)
    gates = gates.reshape(b, l, LANES)
    cos2, sin2 = _rope_tables(l)
    mixed = _mixer_pair(
        [_retention(proj, ret_norm.astype(F32), cos2, sin2),
         _ssd(proj, gates, conv_w.astype(F32), conv_b.astype(F32), dt_bias, a_log, d_skip, ssd_norm.astype(F32))],
        b, l, "retention_ssd")
    return _out_proj(mixed.reshape(b * l, -1), w_out.astype(BF16), h)


def _odd_layer(h, b, l, norm_g, w_in, conv_w, conv_b, i_bias, f_bias, mlstm_norm, a_re, a_im, log_dt, b_re, b_im,
               c_re, c_im, d_skip, w_glu, b_glu, w_out):
    gate_lo = 4 * MLSTM_WIDTH
    w_main, w_small = _split_in_proj(w_in, gate_lo, gate_lo + 2 * MLSTM_HEADS)
    proj, gates = _norm_matmul(h, norm_g.astype(F32)[None, :], w_main, w_small)
    proj = proj.reshape(b, l, MAIN_WIDTH)
    gates = gates.reshape(b, l, LANES)
    mixed = _mixer_pair(
        [_mlstm(proj, gates, conv_w.astype(F32), conv_b.astype(F32), i_bias, f_bias, mlstm_norm.astype(F32)),
         _s5(proj, a_re, a_im, log_dt, b_re, b_im, c_re, c_im, d_skip, w_glu, b_glu)],
        b, l, "mlstm_s5")
    return _out_proj(mixed.reshape(b * l, -1), w_out.astype(BF16), h)


def kernel(x, even_norm, even_w_in, ret_norm, ssd_conv_w, ssd_conv_b, ssd_dt_bias, ssd_a_log, ssd_d, ssd_norm, even_w_out, odd_norm, odd_w_in, mlstm_conv_w, mlstm_conv_b, mlstm_i_bias, mlstm_f_bias, mlstm_norm, s5_a_re, s5_a_im, s5_log_dt, s5_b_re, s5_b_im, s5_c_re, s5_c_im, s5_d, s5_w_glu, s5_b_glu, odd_w_out, moe_norm, moe_router_g, moe_router_g_b, moe_router_e, moe_router_e_b, moe_w_gate, moe_w_up, moe_w_down, final_norm):
    b, l, d = x.shape
    depth = moe_norm.shape[0]
    h = x.reshape(b * l, d)
    for layer in range(depth):
        i = layer // 2
        if layer % 2 == 0:
            h = _even_layer(h, b, l, even_norm[i], even_w_in[i], ret_norm[i], ssd_conv_w[i], ssd_conv_b[i],
                            ssd_dt_bias[i], ssd_a_log[i], ssd_d[i], ssd_norm[i], even_w_out[i])
        else:
            h = _odd_layer(h, b, l, odd_norm[i], odd_w_in[i], mlstm_conv_w[i], mlstm_conv_b[i], mlstm_i_bias[i],
                           mlstm_f_bias[i], mlstm_norm[i], s5_a_re[i], s5_a_im[i], s5_log_dt[i], s5_b_re[i],
                           s5_b_im[i], s5_c_re[i], s5_c_im[i], s5_d[i], s5_w_glu[i], s5_b_glu[i], odd_w_out[i])
        h = _moe(h, layer, moe_norm[layer], moe_router_g[layer], moe_router_g_b[layer], moe_router_e[layer],
                 moe_router_e_b[layer], moe_w_gate, moe_w_up, moe_w_down, final_norm,
                 final_norm=(layer == depth - 1))
    return h.reshape(b, l, d)
```

```python
import functools
import math

import numpy as np
import jax
import jax.numpy as jnp
from jax import lax
from jax.experimental import pallas as pl
from jax.experimental.pallas import tpu as pltpu

F32 = jnp.float32
BF16 = jnp.bfloat16
HIGHEST = lax.Precision.HIGHEST

D_MODEL = 2048
CHUNK = 128
CONV_W = 4
EPS = 1e-6
LANES = 128
CONV_PAD = 8

RET_HEADS = 8
RET_DIM = 128
RET_WIDTH = RET_HEADS * RET_DIM
ROPE_BASE = 10000.0
SSD_HEAD_DIM = 64
SSD_HEADS = 16
SSD_WIDTH = SSD_HEADS * SSD_HEAD_DIM
SSD_GROUPS = 2
SSD_HPG = SSD_HEADS // SSD_GROUPS
SSD_STATE = 128
SSD_GW = SSD_WIDTH // SSD_GROUPS
MAIN_WIDTH = 6656
MLSTM_HEADS = 4
MLSTM_DIM = 384
MLSTM_WIDTH = MLSTM_HEADS * MLSTM_DIM
S5_GROUP = 16
S5_GROUPS = 32
S5_WIDTH = S5_GROUP * S5_GROUPS
S5_STATE = 64
S5_LANES = S5_GROUPS * S5_STATE
MOE_GROUPS = 4
EXPERTS_PER_GROUP = 8
N_EXPERTS = MOE_GROUPS * EXPERTS_PER_GROUP
MOE_TOP_K = 2
EXPERT_FF = 512
MOE_ROWS = 256
COMBINE_ROWS = 512

VMEM_LIMIT = 56 * 1024 * 1024

IN_PROJ_TILE = (512, 6656)
OUT_PROJ_TILE = (1024, 2048)
ROUTER_ROWS = 512
DISPATCH_ROWS = 1024


def _cparams(sem):
    return pltpu.CompilerParams(dimension_semantics=sem, vmem_limit_bytes=VMEM_LIMIT)


def _dot(a, b):
    return jnp.dot(a, b, preferred_element_type=F32)


def _dot_nt(a, b):
    return lax.dot_general(a, b, (((1,), (1,)), ((), ())), preferred_element_type=F32)


def _dot_tn(a, b):
    return lax.dot_general(a, b, (((0,), (0,)), ((), ())), preferred_element_type=F32)


def _dot_exact(a, b):
    return jnp.dot(a, b, preferred_element_type=F32, precision=HIGHEST)


def _sigmoid(x):
    return 0.5 * jnp.tanh(0.5 * x) + 0.5


def _silu(x):
    return x * _sigmoid(x)


def _softplus(x):
    return jnp.maximum(x, 0.0) + jnp.log1p(jnp.exp(-jnp.abs(x)))


def _tril_f32(n):
    r = lax.broadcasted_iota(jnp.int32, (n, n), 0)
    c = lax.broadcasted_iota(jnp.int32, (n, n), 1)
    return (r >= c).astype(F32)


def _norm_matmul_kernel(x_ref, g_ref, w_ref, w2_ref, o_ref, o2_ref, xn_ref):
    @pl.when(pl.program_id(1) == 0)
    def _():
        x = x_ref[...]
        ms = jnp.mean(x * x, axis=-1, keepdims=True)
        xn = (x * lax.rsqrt(ms + EPS) * g_ref[...]).astype(BF16)
        xn_ref[...] = xn
        o2_ref[...] = _dot(xn, w2_ref[...])

    o_ref[...] = _dot(xn_ref[...], w_ref[...]).astype(o_ref.dtype)


def _norm_matmul(x, g, w, w2):
    t, d = x.shape
    n = w.shape[1]
    tm, tn = min(IN_PROJ_TILE[0], t), IN_PROJ_TILE[1]
    return pl.pallas_call(
        _norm_matmul_kernel,
        grid=(t // tm, n // tn),
        in_specs=[
            pl.BlockSpec((tm, d), lambda i, j: (i, 0)),
            pl.BlockSpec((1, d), lambda i, j: (0, 0)),
            pl.BlockSpec((d, tn), lambda i, j: (0, j), pipeline_mode=pl.Buffered(1 if n == tn else 2)),
            pl.BlockSpec((d, LANES), lambda i, j: (0, 0)),
        ],
        out_specs=[
            pl.BlockSpec((tm, tn), lambda i, j: (i, j)),
            pl.BlockSpec((tm, LANES), lambda i, j: (i, 0)),
        ],
        out_shape=[jax.ShapeDtypeStruct((t, n), BF16), jax.ShapeDtypeStruct((t, LANES), F32)],
        scratch_shapes=[pltpu.VMEM((tm, d), BF16)],
        compiler_params=_cparams(("parallel", "arbitrary")),
        name="norm_in_proj",
    )(x, g, w, w2)


def _out_proj_kernel(a_ref, w_ref, r_ref, o_ref):
    o_ref[...] = r_ref[...] + _dot(a_ref[...], w_ref[...])


def _out_proj(a, w, res):
    t, k = a.shape
    n = w.shape[1]
    tm, tn = min(OUT_PROJ_TILE[0], t), OUT_PROJ_TILE[1]
    return pl.pallas_call(
        _out_proj_kernel,
        grid=(t // tm, n // tn),
        in_specs=[
            pl.BlockSpec((tm, k), lambda i, j: (i, 0)),
            pl.BlockSpec((k, tn), lambda i, j: (0, j), pipeline_mode=pl.Buffered(1 if n == tn else 2)),
            pl.BlockSpec((tm, tn), lambda i, j: (i, j)),
        ],
        out_specs=pl.BlockSpec((tm, tn), lambda i, j: (i, j)),
        out_shape=jax.ShapeDtypeStruct((t, n), F32),
        compiler_params=_cparams(("parallel", "arbitrary")),
        name="out_proj",
    )(a, w, res)


_RET_LOG_GAMMA = [math.log1p(-(2.0 ** (-5.0 - h))) for h in range(RET_HEADS)]


def _rope_tables(seq):
    inv = 1.0 / (ROPE_BASE ** (np.arange(0, RET_DIM, 2, dtype=np.float64) / RET_DIM))
    ang = np.arange(seq, dtype=np.float64)[:, None] * inv[None, :]
    cos = np.cos(ang)
    sin = np.sin(ang)
    cos2 = np.concatenate([cos, cos], axis=1).astype(np.float32)
    sin2 = np.concatenate([-sin, sin], axis=1).astype(np.float32)
    return jnp.asarray(cos2), jnp.asarray(sin2)


def _head_layer_norm(t, g):
    mu = jnp.mean(t, axis=-1, keepdims=True)
    tc = t - mu
    var = jnp.mean(tc * tc, axis=-1, keepdims=True)
    return tc * lax.rsqrt(var + EPS) * g


def _retention_init(st_ref):
    st_ref[...] = jnp.zeros_like(st_ref)


def _retention_body(q_ref, k_ref, v_ref, g_ref, cos_ref, sin_ref, ng_ref, o_ref, st_ref):
    cos = cos_ref[...]
    sin = sin_ref[...]
    row = lax.broadcasted_iota(jnp.int32, (CHUNK, CHUNK), 0)
    col = lax.broadcasted_iota(jnp.int32, (CHUNK, CHUNK), 1)
    rel = (row - col).astype(F32)
    causal = row >= col
    pos = lax.broadcasted_iota(jnp.int32, (CHUNK, 1), 0).astype(F32)
    scale = RET_DIM ** -0.5
    for h in range(RET_HEADS):
        lg = _RET_LOG_GAMMA[h]
        sl = slice(h * RET_DIM, (h + 1) * RET_DIM)
        q = q_ref[:, sl].astype(F32)
        k = k_ref[:, sl].astype(F32)
        q = q * cos + pltpu.roll(q, RET_DIM // 2, 1) * sin
        k = (k * cos + pltpu.roll(k, RET_DIM // 2, 1) * sin) * scale
        qb = q.astype(BF16)
        kb = k.astype(BF16)
        vb = v_ref[:, sl].astype(BF16)
        decay = jnp.where(causal, jnp.exp(lg * jnp.maximum(rel, 0.0)), 0.0)
        scores = _dot_nt(qb, kb) * decay
        state = st_ref[h]
        out = _dot(scores.astype(BF16), vb)
        out = out + _dot(qb, state.astype(BF16)) * jnp.exp(lg * (pos + 1.0))
        kd = (k * jnp.exp(lg * (CHUNK - 1.0 - pos))).astype(BF16)
        st_ref[h] = math.exp(lg * CHUNK) * state + _dot_tn(kd, vb)
        y = _head_layer_norm(out, ng_ref[h:h + 1, :]) * _silu(g_ref[:, sl].astype(F32))
        o_ref[:, sl] = y.astype(o_ref.dtype)


class _MixerPart:
    def __init__(self, init, body, width, in_specs, args, scratch):
        self.init, self.body, self.width = init, body, width
        self.in_specs, self.args, self.scratch = in_specs, args, scratch


def _mixer_pair_kernel(*refs, parts):
    n_in = sum(len(p.args) for p in parts)
    o_ref = refs[n_in]
    ins, scr, lanes = [], [], []
    i0, s0, c0 = 0, n_in + 1, 0
    for p in parts:
        ins.append(refs[i0:i0 + len(p.args)])
        scr.append(refs[s0:s0 + len(p.scratch)])
        lanes.append((c0, c0 + p.width))
        i0, s0, c0 = i0 + len(p.args), s0 + len(p.scratch), c0 + p.width

    @pl.when(pl.program_id(1) == 0)
    def _():
        for p, s in zip(parts, scr):
            p.init(*s)

    for p, a, s, (lo, hi) in zip(parts, ins, scr, lanes):
        p.body(*a, o_ref.at[:, lo:hi], *s)


def _mixer_pair(parts, b, l, name):
    width = sum(p.width for p in parts)
    return pl.pallas_call(
        functools.partial(_mixer_pair_kernel, parts=parts),
        grid=(b, l // CHUNK),
        in_specs=[s for p in parts for s in p.in_specs],
        out_specs=pl.BlockSpec((None, CHUNK, width), lambda bi, c: (bi, c, 0)),
        out_shape=jax.ShapeDtypeStruct((b, l, width), BF16),
        scratch_shapes=[s for p in parts for s in p.scratch],
        compiler_params=_cparams(("parallel", "arbitrary")),
        name=name,
    )(*[a for p in parts for a in p.args])


def _retention(proj, norm_g, cos2, sin2):
    col_spec = lambda j: pl.BlockSpec((None, CHUNK, RET_WIDTH), lambda bi, c: (bi, c, j))
    tab_spec = pl.BlockSpec((CHUNK, RET_DIM), lambda bi, c: (c, 0))
    return _MixerPart(
        _retention_init, _retention_body, RET_WIDTH,
        [col_spec(0), col_spec(1), col_spec(2), col_spec(3), tab_spec, tab_spec,
         pl.BlockSpec((RET_HEADS, RET_DIM), lambda bi, c: (0, 0))],
        (proj, proj, proj, proj, cos2, sin2, norm_g),
        [pltpu.VMEM((RET_HEADS, RET_DIM, RET_DIM), F32)])


def _conv_reset(buf_ref):
    buf_ref[0:CONV_PAD, :] = jnp.zeros((CONV_PAD, buf_ref.shape[1]), F32)


def _conv_silu(x, buf_ref, w_ref, b_ref):
    n = x.shape[0]
    buf_ref[CONV_PAD:CONV_PAD + n, :] = x
    acc = b_ref[...] + w_ref[CONV_W - 1:CONV_W, :] * x
    for j in range(CONV_W - 1):
        off = CONV_PAD - (CONV_W - 1) + j
        acc = acc + w_ref[j:j + 1, :] * buf_ref[off:off + n, :]
    buf_ref[0:CONV_PAD, :] = x[n - CONV_PAD:, :]
    return _silu(acc)


def _ssd_init(st_ref, bufx_ref, bufb_ref):
    st_ref[...] = jnp.zeros_like(st_ref)
    _conv_reset(bufx_ref)
    _conv_reset(bufb_ref)


def _ssd_body(z_ref, xs_ref, bc_ref, dt_ref, cwx_ref, cbx_ref, cwb_ref, cbb_ref, dtb_ref, alog_ref,
              dskip_ref, ng_ref, o_ref, st_ref, bufx_ref, bufb_ref):
    xs = _conv_silu(xs_ref[...].astype(F32), bufx_ref, cwx_ref, cbx_ref)
    bc = _conv_silu(bc_ref[...].astype(F32), bufb_ref, cwb_ref, cbb_ref)
    dt = _softplus(dt_ref[...] + dtb_ref[...])
    da = dt * (-jnp.exp(alog_ref[...]))
    a_cum = _dot_exact(_tril_f32(CHUNK), da)
    a_cum_t = a_cum.T
    a_last = a_cum[CHUNK - 1:CHUNK, :]
    e_cum = jnp.exp(a_cum)
    to_end = jnp.exp(a_last - a_cum)
    e_last = jnp.exp(a_last)
    spread = jnp.where(lax.broadcasted_iota(jnp.int32, (LANES, SSD_WIDTH), 1) // SSD_HEAD_DIM
                       == lax.broadcasted_iota(jnp.int32, (LANES, SSD_WIDTH), 0), 1.0, 0.0)
    xdt_all = xs * _dot_exact(dt, spread)
    xend_all = xdt_all * _dot_exact(to_end, spread)
    e_cum_all = _dot_exact(e_cum, spread)
    row = lax.broadcasted_iota(jnp.int32, (CHUNK, CHUNK), 0)
    col = lax.broadcasted_iota(jnp.int32, (CHUNK, CHUNK), 1)
    causal = row >= col
    gn = SSD_GROUPS * SSD_STATE
    for g in range(SSD_GROUPS):
        bm = bc[:, g * SSD_STATE:(g + 1) * SSD_STATE].astype(BF16)
        cm = bc[:, gn + g * SSD_STATE:gn + (g + 1) * SSD_STATE].astype(BF16)
        cb = _dot_nt(cm, bm)
        ys = []
        for r in range(SSD_HPG):
            hd = g * SSD_HPG + r
            sl = slice(hd * SSD_HEAD_DIM, (hd + 1) * SSD_HEAD_DIM)
            seg = jnp.exp(jnp.where(causal, a_cum[:, hd:hd + 1] - a_cum_t[hd:hd + 1, :], -jnp.inf))
            state = st_ref[hd]
            y = _dot((cb * seg).astype(BF16), xdt_all[:, sl].astype(BF16))
            y = y + _dot(cm, state.astype(BF16)) * e_cum_all[:, sl]
            st_ref[hd] = e_last[:, hd:hd + 1] * state + _dot_tn(bm, xend_all[:, sl].astype(BF16))
            ys.append(y + dskip_ref[:, sl] * xs[:, sl])
        gsl = slice(g * SSD_GW, (g + 1) * SSD_GW)
        y = jnp.concatenate(ys, axis=1) * _silu(z_ref[:, gsl].astype(F32))
        y = y * lax.rsqrt(jnp.mean(y * y, axis=-1, keepdims=True) + EPS)
        o_ref[:, gsl] = (y * ng_ref[:, gsl]).astype(o_ref.dtype)


def _ssd(proj, gates, conv_w, conv_b, dt_bias, a_log, d_skip, norm_g):
    bcw = 2 * SSD_GROUPS * SSD_STATE
    pad = lambda v: jnp.pad(v.astype(F32), (0, LANES - v.shape[0]))[None, :]
    full = lambda shape: pl.BlockSpec(shape, lambda bi, c: (0,) * len(shape))
    z_off = 4 * RET_WIDTH // SSD_WIDTH
    return _MixerPart(
        _ssd_init, _ssd_body, SSD_WIDTH,
        [
            pl.BlockSpec((None, CHUNK, SSD_WIDTH), lambda bi, c: (bi, c, z_off)),
            pl.BlockSpec((None, CHUNK, SSD_WIDTH), lambda bi, c: (bi, c, z_off + 1)),
            pl.BlockSpec((None, CHUNK, bcw), lambda bi, c: (bi, c, (z_off + 2) * SSD_WIDTH // bcw)),
            pl.BlockSpec((None, CHUNK, LANES), lambda bi, c: (bi, c, 0)),
            full((CONV_W, SSD_WIDTH)), full((1, SSD_WIDTH)), full((CONV_W, bcw)), full((1, bcw)),
            full((1, LANES)), full((1, LANES)), full((1, SSD_WIDTH)), full((1, SSD_WIDTH)),
        ],
        (proj, proj, proj, gates,
         conv_w[:, :SSD_WIDTH], conv_b[None, :SSD_WIDTH], conv_w[:, SSD_WIDTH:], conv_b[None, SSD_WIDTH:],
         pad(dt_bias), pad(a_log), jnp.repeat(d_skip.astype(F32), SSD_HEAD_DIM)[None, :], norm_g[None, :]),
        [
            pltpu.VMEM((SSD_HEADS, SSD_STATE, SSD_HEAD_DIM), F32),
            pltpu.VMEM((CONV_PAD + CHUNK, SSD_WIDTH), F32),
            pltpu.VMEM((CONV_PAD + CHUNK, bcw), F32),
        ])


def _mlstm_init(c_ref, n_ref, m_ref, bufq_ref, bufk_ref):
    c_ref[...] = jnp.zeros_like(c_ref)
    n_ref[...] = jnp.zeros_like(n_ref)
    m_ref[...] = jnp.full(m_ref.shape, -jnp.inf, F32)
    _conv_reset(bufq_ref)
    _conv_reset(bufk_ref)


def _mlstm_body(q_ref, k_ref, v_ref, o_ref, gt_ref, cwq_ref, cbq_ref, cwk_ref, cbk_ref, gb_ref, ng_ref,
                out_ref, c_ref, n_ref, m_ref, bufq_ref, bufk_ref):
    nh = MLSTM_HEADS
    pre = gt_ref[...] + gb_ref[...]
    bcum_all = _dot_exact(_tril_f32(CHUNK), -_softplus(-pre))
    pre_t = pre.T
    bcum_all_t = bcum_all.T
    row = lax.broadcasted_iota(jnp.int32, (CHUNK, CHUNK), 0)
    col = lax.broadcasted_iota(jnp.int32, (CHUNK, CHUNK), 1)
    causal = row >= col

    for hd in range(nh):
        sl = slice(hd * MLSTM_DIM, (hd + 1) * MLSTM_DIM)
        q = _conv_silu(q_ref[:, sl].astype(F32), bufq_ref.at[:, sl], cwq_ref.at[:, sl], cbq_ref.at[:, sl])
        q = q * (MLSTM_DIM ** -0.5)
        k = _conv_silu(k_ref[:, sl].astype(F32), bufk_ref.at[:, sl], cwk_ref.at[:, sl], cbk_ref.at[:, sl])
        qb = q.astype(BF16)
        kb = k.astype(BF16)
        vb = v_ref[:, sl].astype(BF16)
        ig = pre[:, hd:hd + 1]
        ig_t = pre_t[hd:hd + 1, :]
        bcum = bcum_all[:, nh + hd:nh + hd + 1]
        bcum_t = bcum_all_t[nh + hd:nh + hd + 1, :]

        log_d = jnp.where(causal, bcum - bcum_t + ig_t, -jnp.inf)
        m_prev = m_ref[hd]
        m_inter = bcum + m_prev
        m_t = jnp.maximum(jnp.max(log_d, axis=1, keepdims=True), m_inter)
        scores = _dot_nt(qb, kb) * jnp.exp(log_d - m_t)
        inter_scale = jnp.exp(m_inter - m_t)
        c_st = c_ref[hd]
        n_st = n_ref[hd]
        num = _dot(scores.astype(BF16), vb) + inter_scale * _dot(qb, c_st.astype(BF16))
        den = jnp.sum(scores, axis=1, keepdims=True) + inter_scale * jnp.sum(q * n_st, axis=1, keepdims=True)
        hval = num / jnp.maximum(jnp.abs(den), jnp.exp(-m_t))

        b_last = bcum[CHUNK - 1:CHUNK, :]
        w_log = b_last - bcum + ig
        m_new = jnp.maximum(b_last + m_prev, jnp.max(w_log, axis=0, keepdims=True))
        kw = k * jnp.exp(w_log - m_new)
        prev_scale = jnp.exp(b_last + m_prev - m_new)
        c_ref[hd] = prev_scale * c_st + _dot_tn(kw.astype(BF16), vb)
        n_ref[hd] = prev_scale * n_st + jnp.sum(kw, axis=0, keepdims=True)
        m_ref[hd] = m_new

        y = _head_layer_norm(hval, ng_ref[:, sl]) * _sigmoid(o_ref[:, sl].astype(F32))
        out_ref[:, sl] = y.astype(out_ref.dtype)


def _mlstm(proj, gates, conv_w, conv_b, i_bias, f_bias, norm_g):
    nh = MLSTM_HEADS
    w = MLSTM_WIDTH
    col_spec = lambda j: pl.BlockSpec((None, CHUNK, w), lambda bi, c: (bi, c, j))
    cw_spec = lambda j: pl.BlockSpec((CONV_W, w), lambda bi, c: (0, j))
    cb_spec = lambda j: pl.BlockSpec((1, w), lambda bi, c: (0, j))
    gate_bias = jnp.pad(jnp.concatenate([i_bias, f_bias]).astype(F32), (0, LANES - 2 * nh))[None, :]
    return _MixerPart(
        _mlstm_init, _mlstm_body, w,
        [
            col_spec(0), col_spec(1), col_spec(2), col_spec(3),
            pl.BlockSpec((None, CHUNK, LANES), lambda bi, c: (bi, c, 0)),
            cw_spec(0), cb_spec(0), cw_spec(1), cb_spec(1),
            pl.BlockSpec((1, LANES), lambda bi, c: (0, 0)),
            pl.BlockSpec((1, w), lambda bi, c: (0, 0)),
        ],
        (proj, proj, proj, proj, gates, conv_w, conv_b[None, :], conv_w, conv_b[None, :], gate_bias,
         norm_g.reshape(1, w)),
        [
            pltpu.VMEM((nh, MLSTM_DIM, MLSTM_DIM), F32),
            pltpu.VMEM((nh, 1, MLSTM_DIM), F32),
            pltpu.VMEM((nh, 1, 1), F32),
            pltpu.VMEM((CONV_PAD + CHUNK, w), F32),
            pltpu.VMEM((CONV_PAD + CHUNK, w), F32),
        ])


S5_SEG = CHUNK // CONV_PAD
S5_SLAB = 128
S5_SLAB_LANES = S5_SLAB // S5_GROUP * S5_STATE
_S5_SEG_LOG = int(math.log2(CONV_PAD))


def _cmul(a_re, a_im, b_re, b_im):
    return a_re * b_re - a_im * b_im, a_re * b_im + a_im * b_re


def _s5_init(sre_ref, sim_ref, st_re_ref, st_im_ref, ucol_ref):
    st_re_ref[...] = jnp.zeros_like(st_re_ref)
    st_im_ref[...] = jnp.zeros_like(st_im_ref)


def _s5_body(u_ref, bre_ref, bim_ref, pt_re_ref, pt_im_ref, sg_re_ref, sg_im_ref, cre_ref, cim_ref,
             dskip_ref, wglu_ref, bglu_ref, o_ref, sre_ref, sim_ref, st_re_ref, st_im_ref, ucol_ref):
    for q in range(S5_WIDTH // LANES):
        ucol_ref[q] = u_ref[:, q * LANES:(q + 1) * LANES].astype(F32)
    u = jnp.concatenate(
        [jnp.concatenate([ucol_ref[q, pl.ds(v, CONV_PAD, stride=S5_SEG), :] for v in range(S5_SEG)], axis=0)
         for q in range(S5_WIDTH // LANES)], axis=1)
    ub = u.astype(BF16)
    sub = lax.broadcasted_iota(jnp.int32, (CONV_PAD, LANES), 0)
    grp = lambda x, v: x[v * CONV_PAD:(v + 1) * CONV_PAD, :]
    for jj in range(S5_WIDTH // S5_SLAB):
        ch = slice(jj * S5_SLAB, (jj + 1) * S5_SLAB)
        xr = _dot(ub[:, ch], bre_ref[ch, jj * S5_SLAB_LANES:(jj + 1) * S5_SLAB_LANES])
        xi = _dot(ub[:, ch], bim_ref[ch, jj * S5_SLAB_LANES:(jj + 1) * S5_SLAB_LANES])
        for q in range(S5_SLAB_LANES // LANES):
            sl = slice(jj * S5_SLAB_LANES + q * LANES, jj * S5_SLAB_LANES + (q + 1) * LANES)
            lre = xr[:, q * LANES:(q + 1) * LANES]
            lim = xi[:, q * LANES:(q + 1) * LANES]
            a_re = pt_re_ref[0:CONV_PAD, sl]
            a_im = pt_im_ref[0:CONV_PAD, sl]
            s_re = [grp(lre, 0)]
            s_im = [grp(lim, 0)]
            for v in range(1, S5_SEG):
                d_re, d_im = _cmul(a_re, a_im, s_re[-1], s_im[-1])
                s_re.append(grp(lre, v) + d_re)
                s_im.append(grp(lim, v) + d_im)
            c_re = st_re_ref[:, sl]
            c_im = st_im_ref[:, sl]
            i_re, i_im = _cmul(sg_re_ref[0:1, sl], sg_im_ref[0:1, sl], c_re, c_im)
            e_re = s_re[-1] + jnp.where(sub == 0, i_re, 0.0)
            e_im = s_im[-1] + jnp.where(sub == 0, i_im, 0.0)
            for k in range(_S5_SEG_LOG):
                d = 1 << k
                h_re = jnp.where(sub >= d, pltpu.roll(e_re, d, 0), 0.0)
                h_im = jnp.where(sub >= d, pltpu.roll(e_im, d, 0), 0.0)
                d_re, d_im = _cmul(sg_re_ref[k:k + 1, sl], sg_im_ref[k:k + 1, sl], h_re, h_im)
                e_re = e_re + d_re
                e_im = e_im + d_im
            st_re_ref[:, sl] = e_re[CONV_PAD - 1:CONV_PAD, :]
            st_im_ref[:, sl] = e_im[CONV_PAD - 1:CONV_PAD, :]
            p_re = jnp.where(sub >= 1, pltpu.roll(e_re, 1, 0), c_re)
            p_im = jnp.where(sub >= 1, pltpu.roll(e_im, 1, 0), c_im)
            for v in range(S5_SEG):
                rows = slice(v * CONV_PAD, (v + 1) * CONV_PAD)
                d_re, d_im = _cmul(pt_re_ref[rows, sl], pt_im_ref[rows, sl], p_re, p_im)
                s_re[v] = s_re[v] + d_re
                s_im[v] = s_im[v] + d_im
            sre_ref[:, sl] = jnp.concatenate(s_re, axis=0).astype(BF16)
            sim_ref[:, sl] = jnp.concatenate(s_im, axis=0).astype(BF16)

    ys = []
    for jj in range(S5_WIDTH // S5_SLAB):
        ch = slice(jj * S5_SLAB, (jj + 1) * S5_SLAB)
        ln = slice(jj * S5_SLAB_LANES, (jj + 1) * S5_SLAB_LANES)
        ys.append(_dot(sre_ref[:, ln], cre_ref[ln, ch]) - _dot(sim_ref[:, ln], cim_ref[ln, ch]))
    y = jnp.concatenate(ys, axis=1) + dskip_ref[...] * u
    y = 0.5 * y * (1.0 + jnp.tanh(math.sqrt(2.0 / math.pi) * (y + 0.044715 * (y * y * y))))
    vg = _dot(y.astype(BF16), wglu_ref[...]) + bglu_ref[...]
    res = (vg[:, :S5_WIDTH] * _sigmoid(vg[:, S5_WIDTH:])).astype(BF16)
    t_idx = lax.broadcasted_iota(jnp.int32, (CHUNK, CHUNK), 0)
    p_idx = lax.broadcasted_iota(jnp.int32, (CHUNK, CHUNK), 1)
    unperm = jnp.where(p_idx == CONV_PAD * (t_idx % S5_SEG) + t_idx // S5_SEG, 1.0, 0.0).astype(BF16)
    o_ref[...] = _dot(unperm, res).astype(o_ref.dtype)


def _s5_params(a_re, a_im, log_dt, b_re, b_im, c_re, c_im):
    a_re = a_re.astype(F32)
    a_im = a_im.astype(F32)
    dt = jnp.exp(log_dt.astype(F32))[:, None]
    mag = jnp.exp(a_re * dt)
    abar_re = mag * jnp.cos(a_im * dt)
    abar_im = mag * jnp.sin(a_im * dt)
    den = a_re * a_re + a_im * a_im
    nr = abar_re - 1.0
    coef_re = (nr * a_re + abar_im * a_im) / den
    coef_im = (abar_im * a_re - nr * a_im) / den
    b_re = b_re.astype(F32)
    b_im = b_im.astype(F32)
    bbar_re = coef_re[..., None] * b_re - coef_im[..., None] * b_im
    bbar_im = coef_re[..., None] * b_im + coef_im[..., None] * b_re
    eye = jnp.eye(S5_GROUPS, dtype=F32)
    in_map = lambda m: jnp.einsum('gnc,gh->gchn', m, eye).reshape(S5_WIDTH, S5_LANES).astype(BF16)
    out_map = lambda m: jnp.einsum('gcn,gh->gnhc', m.astype(F32), eye).reshape(S5_LANES, S5_WIDTH).astype(BF16)

    def powers(exps):
        e = exps[:, None, None]
        m = jnp.exp(e * (a_re * dt)[None])
        ang = e * (a_im * dt)[None]
        return (m * jnp.cos(ang)).reshape(-1, S5_LANES), (m * jnp.sin(ang)).reshape(-1, S5_LANES)

    pt_re, pt_im = powers(jnp.repeat(jnp.arange(1, S5_SEG + 1, dtype=F32), CONV_PAD))
    sg_re, sg_im = powers(jnp.asarray([float(S5_SEG << k) for k in range(_S5_SEG_LOG)], F32))
    pad8 = lambda m: jnp.pad(m, ((0, CONV_PAD - m.shape[0]), (0, 0)))
    return (in_map(bbar_re), in_map(bbar_im), pt_re, pt_im, pad8(sg_re), pad8(sg_im),
            out_map(c_re), out_map(c_im))


def _s5(proj, a_re, a_im, log_dt, b_re, b_im, c_re, c_im, d_skip, w_glu, b_glu):
    tabs = _s5_params(a_re, a_im, log_dt, b_re, b_im, c_re, c_im)
    full = lambda shape: pl.BlockSpec(shape, lambda bi, c: (0,) * len(shape))
    u_off = 4 * MLSTM_WIDTH // S5_WIDTH
    return _MixerPart(
        _s5_init, _s5_body, S5_WIDTH,
        [
            pl.BlockSpec((None, CHUNK, S5_WIDTH), lambda bi, c: (bi, c, u_off)),
            full((S5_WIDTH, S5_LANES)), full((S5_WIDTH, S5_LANES)),
            full((CHUNK, S5_LANES)), full((CHUNK, S5_LANES)),
            full((CONV_PAD, S5_LANES)), full((CONV_PAD, S5_LANES)),
            full((S5_LANES, S5_WIDTH)), full((S5_LANES, S5_WIDTH)),
            full((1, S5_WIDTH)), full((S5_WIDTH, 2 * S5_WIDTH)), full((1, 2 * S5_WIDTH)),
        ],
        (proj, *tabs, d_skip.astype(F32).reshape(1, S5_WIDTH), w_glu.astype(BF16), b_glu.astype(F32)[None, :]),
        [
            pltpu.VMEM((CHUNK, S5_LANES), BF16), pltpu.VMEM((CHUNK, S5_LANES), BF16),
            pltpu.VMEM((1, S5_LANES), F32), pltpu.VMEM((1, S5_LANES), F32),
            pltpu.VMEM((S5_WIDTH // LANES, CHUNK, LANES), F32),
        ])


def _router_kernel(x_ref, g_ref, whi_ref, wlo_ref, b_ref, ids_ref, wts_ref, cnt_ref, carry_ref):
    @pl.when(pl.program_id(0) == 0)
    def _():
        carry_ref[...] = jnp.zeros_like(carry_ref)

    x = x_ref[...]
    xn = x * lax.rsqrt(jnp.mean(x * x, axis=-1, keepdims=True) + EPS) * g_ref[...]
    xh = xn.astype(BF16)
    xl = (xn - xh.astype(F32)).astype(BF16)
    logits = _dot(xh, whi_ref[...]) + (_dot(xl, whi_ref[...]) + _dot(xh, wlo_ref[...])) + b_ref[...]
    lane = lax.broadcasted_iota(jnp.int32, logits.shape, 1)
    big = jnp.int32(LANES)
    neg = -jnp.inf

    def top(vals):
        m = jnp.max(vals, axis=1, keepdims=True)
        idx = jnp.min(jnp.where(vals == m, lane, big), axis=1, keepdims=True)
        return m, idx

    gl = jnp.where(lane < MOE_GROUPS, logits, neg)
    g_max, g_idx = top(gl)
    g_w = 1.0 / jnp.sum(jnp.exp(gl - g_max), axis=1, keepdims=True)
    lo = MOE_GROUPS + EXPERTS_PER_GROUP * g_idx
    el = jnp.where((lane >= lo) & (lane < lo + EXPERTS_PER_GROUP), logits, neg)
    m1, j1 = top(el)
    m2, j2 = top(jnp.where(lane == j1, neg, el))
    e2 = jnp.exp(m2 - m1)
    w1 = g_w / (1.0 + e2)
    w2 = g_w * e2 / (1.0 + e2)
    e1 = j1 - MOE_GROUPS
    e2 = j2 - MOE_GROUPS

    tm = x.shape[0]
    hit = jnp.where(lane == e1, 1.0, jnp.where(lane == e2, 1.0, 0.0))
    row = lax.broadcasted_iota(jnp.int32, (tm, tm), 0)
    col = lax.broadcasted_iota(jnp.int32, (tm, tm), 1)
    before = jnp.where(row > col, 1.0, 0.0).astype(BF16)
    prefix = _dot(before, hit.astype(BF16)) + carry_ref[...]
    r1 = jnp.sum(jnp.where(lane == e1, prefix, 0.0), axis=1, keepdims=True)
    r2 = jnp.sum(jnp.where(lane == e2, prefix, 0.0), axis=1, keepdims=True)
    carry_ref[...] = carry_ref[...] + jnp.sum(hit, axis=0, keepdims=True)
    cnt_ref[...] = carry_ref[...]

    wts_ref[...] = jnp.where(lane == 0, w1, jnp.where(lane == 1, w2, 0.0))
    ids = jnp.where(lane == 0, e1.astype(F32), jnp.where(lane == 1, e2.astype(F32),
                                                         jnp.where(lane == 2, r1, jnp.where(lane == 3, r2, 0.0))))
    for c in range(tm // LANES):
        ids_ref[:, c * LANES:(c + 1) * LANES] = ids[c * LANES:(c + 1) * LANES, :].T[0:CONV_PAD, :].astype(jnp.int32)


def _router(x, g, w, bias):
    t, d = x.shape
    tm = min(ROUTER_ROWS, t)
    w_hi = w.astype(BF16)
    w_lo = (w - w_hi.astype(F32)).astype(BF16)
    return pl.pallas_call(
        _router_kernel,
        grid=(t // tm,),
        in_specs=[
            pl.BlockSpec((tm, d), lambda i: (i, 0)),
            pl.BlockSpec((1, d), lambda i: (0, 0)),
            pl.BlockSpec((d, LANES), lambda i: (0, 0)),
            pl.BlockSpec((d, LANES), lambda i: (0, 0)),
            pl.BlockSpec((1, LANES), lambda i: (0, 0)),
        ],
        out_specs=[pl.BlockSpec((CONV_PAD, tm), lambda i: (0, i)), pl.BlockSpec((tm, LANES), lambda i: (i, 0)),
                   pl.BlockSpec((1, LANES), lambda i: (0, 0))],
        out_shape=[jax.ShapeDtypeStruct((CONV_PAD, t), jnp.int32), jax.ShapeDtypeStruct((t, LANES), F32),
                   jax.ShapeDtypeStruct((1, LANES), F32)],
        scratch_shapes=[pltpu.VMEM((1, LANES), F32)],
        compiler_params=_cparams(("arbitrary",)),
        name="moe_router",
    )(x, g, w_hi, w_lo, bias)


def _row_gather_copy(src_hbm, idx_ref, base, buf_ref, slot, sem_ref, r):
    return pltpu.make_async_copy(src_hbm.at[pl.ds(idx_ref[base + r], 1)], buf_ref.at[slot, pl.ds(r, 1)],
                                 sem_ref.at[slot])


def _start_row_gather(src_hbm, idx_ref, base, buf_ref, slot, sem_ref, rows):
    def body(r, carry):
        _row_gather_copy(src_hbm, idx_ref, base, buf_ref, slot, sem_ref, r).start()
        return carry

    lax.fori_loop(0, rows, body, 0)


def _start_row_gather_inline(src_hbm, idx_ref, base, buf_ref, slot, sem_ref, rows, alternate):
    for r in range(rows):
        _row_gather_copy(src_hbm, idx_ref, base, buf_ref, slot, sem_ref, r).start(priority=r % 2 if alternate else 0)


def _wait_row_gather(src_hbm, buf_ref, slot, sem_ref, rows):
    pltpu.make_async_copy(src_hbm.at[pl.ds(0, rows)], buf_ref.at[slot], sem_ref.at[slot]).wait()


def _dispatch_kernel(d0_ref, d1_ref, padrow_ref, padcnt_ref, x_ref, xs_hbm, sem_ref, zsem_ref, zero_ref):
    i = pl.program_id(0)
    rows = x_ref.shape[0]
    base = i * rows

    def zero_copy(row):
        return pltpu.make_async_copy(zero_ref.at[pl.ds(0, 1)], xs_hbm.at[pl.ds(row, 1)], zsem_ref.at[0])

    def for_pad_rows(fn):
        def per_expert(e, carry):
            first = padrow_ref[e]

            def per_row(j, c):
                fn(first + j)
                return c

            lax.fori_loop(0, padcnt_ref[e], per_row, 0)
            return carry

        lax.fori_loop(0, N_EXPERTS, per_expert, 0)

    def tail_copy(j):
        row = pl.multiple_of(padrow_ref[N_EXPERTS] + j * CONV_PAD, CONV_PAD)
        return pltpu.make_async_copy(zero_ref, xs_hbm.at[pl.ds(row, CONV_PAD)], zsem_ref.at[1])

    def for_tail(fn):
        def per_group(j, carry):
            fn(j)
            return carry

        lax.fori_loop(0, padcnt_ref[N_EXPERTS] // CONV_PAD, per_group, 0)

    @pl.when(i == 0)
    def _():
        zero_ref[...] = jnp.zeros_like(zero_ref)
        for_pad_rows(lambda row: zero_copy(row).start())
        for_tail(lambda j: tail_copy(j).start())

    def body(r, carry):
        src = x_ref.at[pl.ds(r, 1)]
        pltpu.make_async_copy(src, xs_hbm.at[pl.ds(d0_ref[base + r], 1)], sem_ref.at[0]).start()
        pltpu.make_async_copy(src, xs_hbm.at[pl.ds(d1_ref[base + r], 1)], sem_ref.at[1]).start(priority=1)
        return carry

    lax.fori_loop(0, rows, body, 0, unroll=8)
    for k in range(MOE_TOP_K):
        pltpu.make_async_copy(x_ref, xs_hbm.at[pl.ds(0, rows)], sem_ref.at[k]).wait()

    @pl.when(i == pl.num_programs(0) - 1)
    def _():
        for_pad_rows(lambda row: zero_copy(row).wait())
        for_tail(lambda j: tail_copy(j).wait())


def _dispatch(x, dest0, dest1, pad_row, pad_cnt, n_rows):
    t, w = x.shape
    tm = min(DISPATCH_ROWS, t)
    grid_spec = pltpu.PrefetchScalarGridSpec(
        num_scalar_prefetch=4,
        grid=(t // tm,),
        in_specs=[pl.BlockSpec((tm, w), lambda i, *_: (i, 0))],
        out_specs=pl.BlockSpec(memory_space=pl.ANY),
        scratch_shapes=[pltpu.SemaphoreType.DMA((MOE_TOP_K,)), pltpu.SemaphoreType.DMA((2,)),
                        pltpu.VMEM((CONV_PAD, w), x.dtype)],
    )
    return pl.pallas_call(
        _dispatch_kernel,
        grid_spec=grid_spec,
        out_shape=jax.ShapeDtypeStruct((n_rows, w), x.dtype),
        compiler_params=_cparams(("arbitrary",)),
        name="moe_dispatch",
    )(dest0, dest1, pad_row, pad_cnt, x)


def _expert_kernel(be_ref, first_ref, par_ref, nxt_ref, nused_ref, x_ref, g_ref, wg_hbm, wu_hbm, wd_hbm,
                   o_ref, wgf_ref, wuf_ref, wdf_ref, wsem_ref, wgb_ref, wub_ref, wdb_ref, *, layer):
    i = pl.program_id(0)
    n_used = nused_ref[0]

    def weight_copies(e, p):
        return (pltpu.make_async_copy(wg_hbm.at[layer, e], wgf_ref.at[p], wsem_ref.at[0, p]),
                pltpu.make_async_copy(wu_hbm.at[layer, e], wuf_ref.at[p], wsem_ref.at[1, p]),
                pltpu.make_async_copy(wd_hbm.at[layer, e], wdf_ref.at[p], wsem_ref.at[2, p]))

    @pl.when(i == 0)
    def _():
        for c in weight_copies(be_ref[0], 0):
            c.start(priority=1)

    @pl.when((i < n_used) & (first_ref[i] == 1))
    def _():
        p = par_ref[i]
        for c in weight_copies(be_ref[i], p):
            c.wait()
        nxt = nxt_ref[i]

        @pl.when(nxt >= 0)
        def _():
            for c in weight_copies(nxt, 1 - p):
                c.start(priority=1)

        wgb_ref[...] = wgf_ref[p].astype(BF16)
        wub_ref[...] = wuf_ref[p].astype(BF16)
        wdb_ref[...] = wdf_ref[p].astype(BF16)

    @pl.when(i < n_used)
    def _():
        x = x_ref[...]
        xn = (x * lax.rsqrt(jnp.mean(x * x, axis=-1, keepdims=True) + EPS) * g_ref[...]).astype(BF16)
        act = (_silu(_dot(xn, wgb_ref[...])) * _dot(xn, wub_ref[...])).astype(BF16)
        o_ref[...] = _dot(act, wdb_ref[...])

    @pl.when(i >= n_used)
    def _():
        o_ref[...] = jnp.zeros_like(o_ref)


def _expert_ffn(x_sorted, g, tables, n_used, w_gate, w_up, w_down, layer):
    n_rows, d = x_sorted.shape
    n_blocks = n_rows // MOE_ROWS
    ff = w_gate.shape[-1]
    any_spec = pl.BlockSpec(memory_space=pl.ANY)
    grid_spec = pltpu.PrefetchScalarGridSpec(
        num_scalar_prefetch=5,
        grid=(n_blocks,),
        in_specs=[pl.BlockSpec((MOE_ROWS, d), lambda i, be, fi, pa, nx, nu: (jnp.minimum(i, nu[0] - 1), 0)),
                  pl.BlockSpec((1, d), lambda i, *_: (0, 0)), any_spec, any_spec, any_spec],
        out_specs=pl.BlockSpec((MOE_ROWS, d), lambda i, *_: (i, 0)),
        scratch_shapes=[
            pltpu.VMEM((2, d, ff), F32), pltpu.VMEM((2, d, ff), F32), pltpu.VMEM((2, ff, d), F32),
            pltpu.SemaphoreType.DMA((3, 2)),
            pltpu.VMEM((d, ff), BF16), pltpu.VMEM((d, ff), BF16), pltpu.VMEM((ff, d), BF16),
        ],
    )
    return pl.pallas_call(
        functools.partial(_expert_kernel, layer=layer),
        grid_spec=grid_spec,
        out_shape=jax.ShapeDtypeStruct((n_rows, d), F32),
        compiler_params=_cparams(("arbitrary",)),
        name="moe_experts",
    )(*tables, n_used, x_sorted, g, w_gate, w_up, w_down)


def _combine_kernel(d0_ref, d1_ref, y_hbm, h_ref, w_ref, fg_ref, o_ref, buf0_ref, buf1_ref, sem0_ref, sem1_ref, *,
                    final_norm):
    i = pl.program_id(0)
    n = pl.num_programs(0)
    slot = lax.rem(i, 2)
    next_base = lax.rem(i + 1, n) * COMBINE_ROWS

    @pl.when(i == 0)
    def _():
        _start_row_gather(y_hbm, d0_ref, 0, buf0_ref, 0, sem0_ref, COMBINE_ROWS)
        _start_row_gather(y_hbm, d1_ref, 0, buf1_ref, 0, sem1_ref, COMBINE_ROWS)

    _wait_row_gather(y_hbm, buf0_ref, slot, sem0_ref, COMBINE_ROWS)
    _wait_row_gather(y_hbm, buf1_ref, slot, sem1_ref, COMBINE_ROWS)
    _start_row_gather_inline(y_hbm, d0_ref, next_base, buf0_ref, 1 - slot, sem0_ref, COMBINE_ROWS, True)
    _start_row_gather_inline(y_hbm, d1_ref, next_base, buf1_ref, 1 - slot, sem1_ref, COMBINE_ROWS, True)
    w = w_ref[...]
    y = h_ref[...] + (w[:, 0:1] * buf0_ref[slot] + w[:, 1:2] * buf1_ref[slot])
    if final_norm:
        y = y * lax.rsqrt(jnp.mean(y * y, axis=-1, keepdims=True) + EPS) * fg_ref[...]
    o_ref[...] = y

    @pl.when(i == n - 1)
    def _():
        _wait_row_gather(y_hbm, buf0_ref, 1 - slot, sem0_ref, COMBINE_ROWS)
        _wait_row_gather(y_hbm, buf1_ref, 1 - slot, sem1_ref, COMBINE_ROWS)


def _combine(y_rows, dest0, dest1, h, wts, final_g, final_norm):
    t, d = h.shape
    grid_spec = pltpu.PrefetchScalarGridSpec(
        num_scalar_prefetch=2,
        grid=(t // COMBINE_ROWS,),
        in_specs=[
            pl.BlockSpec(memory_space=pl.ANY),
            pl.BlockSpec((COMBINE_ROWS, d), lambda i, d0, d1: (i, 0)),
            pl.BlockSpec((COMBINE_ROWS, LANES), lambda i, d0, d1: (i, 0)),
            pl.BlockSpec((1, d), lambda i, d0, d1: (0, 0)),
        ],
        out_specs=pl.BlockSpec((COMBINE_ROWS, d), lambda i, d0, d1: (i, 0)),
        scratch_shapes=[
            pltpu.VMEM((2, COMBINE_ROWS, d), F32), pltpu.VMEM((2, COMBINE_ROWS, d), F32),
            pltpu.SemaphoreType.DMA((2,)), pltpu.SemaphoreType.DMA((2,)),
        ],
    )
    return pl.pallas_call(
        functools.partial(_combine_kernel, final_norm=final_norm),
        grid_spec=grid_spec,
        out_shape=jax.ShapeDtypeStruct((t, d), F32),
        compiler_params=_cparams(("arbitrary",)),
        name="moe_combine",
    )(dest0, dest1, y_rows, h, wts, final_g)


def _moe(h, layer, norm_g, router_g, router_g_b, router_e, router_e_b, w_gate, w_up, w_down, final_g, final_norm):
    t, d = h.shape
    g = norm_g.astype(F32)[None, :]
    pad_cols = LANES - MOE_GROUPS - N_EXPERTS
    w_r = jnp.pad(jnp.concatenate([router_g, router_e], axis=1).astype(F32), ((0, 0), (0, pad_cols)))
    b_r = jnp.pad(jnp.concatenate([router_g_b, router_e_b]).astype(F32), (0, pad_cols))[None, :]
    ids, wts, cnt = _router(h, g, w_r, b_r)

    n_slots = t * MOE_TOP_K
    n_blocks = (n_slots + N_EXPERTS * (MOE_ROWS - 1) + MOE_ROWS - 1) // MOE_ROWS
    n_rows = n_blocks * MOE_ROWS
    counts = cnt[0, :N_EXPERTS].astype(jnp.int32)
    blocks_of = (counts + MOE_ROWS - 1) // MOE_ROWS
    blk_end = jnp.cumsum(blocks_of)
    blk_start = blk_end - blocks_of
    n_used = blk_end[-1:]
    blk = jnp.arange(n_blocks, dtype=jnp.int32)
    block_expert = jnp.minimum(jnp.sum((blk_end[None, :] <= blk[:, None]).astype(jnp.int32), axis=1), N_EXPERTS - 1)
    first = (blk == blk_start[block_expert]).astype(jnp.int32)
    parity = (jnp.cumsum(first) - 1) % 2
    nxt_blk = blk_end[block_expert]
    nxt_expert = jnp.where(nxt_blk < n_used[0], block_expert[jnp.minimum(nxt_blk, n_blocks - 1)], -1)
    tables = (block_expert, first, parity.astype(jnp.int32), nxt_expert.astype(jnp.int32))

    experts = jnp.arange(N_EXPERTS, dtype=jnp.int32)[:, None, None]
    first_row = jnp.sum(jnp.where(ids[None, 0:MOE_TOP_K] == experts, (blk_start * MOE_ROWS)[:, None, None], 0), axis=0)
    dest = first_row + ids[MOE_TOP_K:2 * MOE_TOP_K]

    pad_row = jnp.concatenate([blk_start * MOE_ROWS + counts, n_used * MOE_ROWS])
    pad_cnt = jnp.concatenate([blocks_of * MOE_ROWS - counts, n_rows - n_used * MOE_ROWS])
    x_sorted = _dispatch(h, dest[0], dest[1], pad_row, pad_cnt, n_rows)
    y_rows = _expert_ffn(x_sorted, g, tables, n_used, w_gate, w_up, w_down, layer)
    return _combine(y_rows, dest[0], dest[1], h, wts, final_g.astype(F32)[None, :], final_norm)


def _split_in_proj(w_in, gate_lo, gate_hi):
    main = jnp.concatenate([w_in[:, :gate_lo], w_in[:, gate_hi:]], axis=1).astype(BF16)
    small = jnp.pad(w_in[:, gate_lo:gate_hi], ((0, 0), (0, LANES - (gate_hi - gate_lo)))).astype(BF16)
    return main, small


def _even_layer(h, b, l, norm_g, w_in, ret_norm, conv_w, conv_b, dt_bias, a_log, d_skip, ssd_norm, w_out):
    w_main, w_small = _split_in_proj(w_in, MAIN_WIDTH, MAIN_WIDTH + SSD_HEADS)
    proj, gates = _norm_matmul(h, norm_g.astype(F32)[None, :], w_main, w_small)
    proj = proj.reshape(b, l, MAIN_WIDTH)
    gates = gates.reshape(b, l, LANES)
    cos2, sin2 = _rope_tables(l)
    mixed = _mixer_pair(
        [_retention(proj, ret_norm.astype(F32), cos2, sin2),
         _ssd(proj, gates, conv_w.astype(F32), conv_b.astype(F32), dt_bias, a_log, d_skip, ssd_norm.astype(F32))],
        b, l, "retention_ssd")
    return _out_proj(mixed.reshape(b * l, -1), w_out.astype(BF16), h)


def _odd_layer(h, b, l, norm_g, w_in, conv_w, conv_b, i_bias, f_bias, mlstm_norm, a_re, a_im, log_dt, b_re, b_im,
               c_re, c_im, d_skip, w_glu, b_glu, w_out):
    gate_lo = 4 * MLSTM_WIDTH
    w_main, w_small = _split_in_proj(w_in, gate_lo, gate_lo + 2 * MLSTM_HEADS)
    proj, gates = _norm_matmul(h, norm_g.astype(F32)[None, :], w_main, w_small)
    proj = proj.reshape(b, l, MAIN_WIDTH)
    gates = gates.reshape(b, l, LANES)
    mixed = _mixer_pair(
        [_mlstm(proj, gates, conv_w.astype(F32), conv_b.astype(F32), i_bias, f_bias, mlstm_norm.astype(F32)),
         _s5(proj, a_re, a_im, log_dt, b_re, b_im, c_re, c_im, d_skip, w_glu, b_glu)],
        b, l, "mlstm_s5")
    return _out_proj(mixed.reshape(b * l, -1), w_out.astype(BF16), h)


def kernel(x, even_norm, even_w_in, ret_norm, ssd_conv_w, ssd_conv_b, ssd_dt_bias, ssd_a_log, ssd_d, ssd_norm, even_w_out, odd_norm, odd_w_in, mlstm_conv_w, mlstm_conv_b, mlstm_i_bias, mlstm_f_bias, mlstm_norm, s5_a_re, s5_a_im, s5_log_dt, s5_b_re, s5_b_im, s5_c_re, s5_c_im, s5_d, s5_w_glu, s5_b_glu, odd_w_out, moe_norm, moe_router_g, moe_router_g_b, moe_router_e, moe_router_e_b, moe_w_gate, moe_w_up, moe_w_down, final_norm):
    b, l, d = x.shape
    depth = moe_norm.shape[0]
    h = x.reshape(b * l, d)
    for layer in range(depth):
        i = layer // 2
        if layer % 2 == 0:
            h = _even_layer(h, b, l, even_norm[i], even_w_in[i], ret_norm[i], ssd_conv_w[i], ssd_conv_b[i],
                            ssd_dt_bias[i], ssd_a_log[i], ssd_d[i], ssd_norm[i], even_w_out[i])
        else:
            h = _odd_layer(h, b, l, odd_norm[i], odd_w_in[i], mlstm_conv_w[i], mlstm_conv_b[i], mlstm_i_bias[i],
                           mlstm_f_bias[i], mlstm_norm[i], s5_a_re[i], s5_a_im[i], s5_log_dt[i], s5_b_re[i],
                           s5_b_im[i], s5_c_re[i], s5_c_im[i], s5_d[i], s5_w_glu[i], s5_b_glu[i], odd_w_out[i])
        h = _moe(h, layer, moe_norm[layer], moe_router_g[layer], moe_router_g_b[layer], moe_router_e[layer],
                 moe_router_e_b[layer], moe_w_gate, moe_w_up, moe_w_down, final_norm,
                 final_norm=(layer == depth - 1))
    return h.reshape(b, l, d)
```
